```python
import jax, jax.numpy as jnp
from jax import lax
import numpy as np

D_MODEL = 2048
BATCH = 2
SEQ = 4096
DEPTH = 4
DEC_BATCH = 128
DEC_SEQ = 8
PAST_LEN = 8192
PAGE_SIZE = 128

N_EVEN = (DEPTH + 1) // 2
N_ODD = DEPTH // 2

GLA_HEADS = 4
GLA_DK = 128
GLA_DV = 256
GLA_GATE_RANK = 16
GLA_TAU = 16.0
GLA_CHUNK = 64
GMLP_GROUPS = 4
GMLP_DG = 256
GMLP_CHUNK = 128
GLA_QK_W = GLA_HEADS * GLA_DK
GLA_V_W = GLA_HEADS * GLA_DV
GMLP_W = GMLP_GROUPS * GMLP_DG
EVEN_IN_WIDTHS = (GLA_QK_W, GLA_QK_W, GLA_V_W, GLA_GATE_RANK, GLA_V_W, GMLP_W, GMLP_W)
EVEN_SPLIT_POINTS = tuple(int(s) for s in np.cumsum(EVEN_IN_WIDTHS)[:-1])
EVEN_IN_W = int(sum(EVEN_IN_WIDTHS))
EVEN_OUT_W = GLA_V_W + GMLP_W
MLA_HEADS = 16
Q_LORA = 512
KV_LORA = 512
QK_NOPE = 128
QK_ROPE = 64
V_HEAD = 128
ROPE_THETA = 10000.0
MLA_SCALE = (QK_NOPE + QK_ROPE) ** -0.5
Q_BLOCK = 128
ODD_IN_W = Q_LORA + KV_LORA + QK_ROPE
PEER_HEADS = 8
PEER_NKEYS = 128
PEER_NEXPERTS = PEER_NKEYS * PEER_NKEYS
PEER_QDIM = 256
PEER_TOPK = 16
PEER_TOK_BLOCK = 256
DN_ALPHA = (2.0 * DEPTH) ** 0.25
DN_BETA = (8.0 * DEPTH) ** -0.25
LN_EPS = 1e-5
RMS_EPS = 1e-6
NEG_INF = -1e30

kernel_name = 'hybrid_gla_gmlp_mla_peer_step'


def layer_norm(x, g, b):
    xf = x.astype(jnp.float32)
    mu = jnp.mean(xf, -1, keepdims=True)
    var = jnp.mean(jnp.square(xf - mu), -1, keepdims=True)
    return ((xf - mu) * lax.rsqrt(var + LN_EPS) * g + b).astype(x.dtype)


def rms_norm(x, g):
    xf = x.astype(jnp.float32)
    return (xf * lax.rsqrt(jnp.mean(xf * xf, -1, keepdims=True) + RMS_EPS) * g).astype(x.dtype)


def rope_tables(pos):
    inv = ROPE_THETA ** (-jnp.arange(0, QK_ROPE, 2, dtype=jnp.float32) / QK_ROPE)
    ang = pos.astype(jnp.float32)[:, None] * inv[None, :]
    return jnp.cos(ang), jnp.sin(ang)


def apply_rope(x, cos, sin):
    x1, x2 = jnp.split(x.astype(jnp.float32), 2, axis=-1)
    return jnp.concatenate([x1 * cos - x2 * sin, x1 * sin + x2 * cos], -1).astype(x.dtype)


def gla_scan(q, k, v, log_a, s0):
    B, L, H, DK = q.shape
    C = min(GLA_CHUNK, L)
    n = -(-L // C)
    pad = n * C - L
    if pad:
        pw = ((0, 0), (0, pad), (0, 0), (0, 0))
        q, k, v, log_a = [jnp.pad(t, pw) for t in (q, k, v, log_a)]

    def chunks(t):
        return t.reshape(B, n, C, H, t.shape[-1]).transpose(1, 0, 3, 2, 4).astype(jnp.float32)

    qc, kc, vc, gc = chunks(q), chunks(k), chunks(v), chunks(log_a)
    causal = jnp.tril(jnp.ones((C, C), bool))

    def step(S, inp):
        qi, ki, vi, gi = inp
        b = jnp.cumsum(gi, axis=2)
        b_last = b[:, :, -1:, :]
        q_g = qi * jnp.exp(b)
        k_g = ki * jnp.exp(-b)
        att = jnp.where(causal, jnp.einsum('bhid,bhjd->bhij', q_g, k_g), 0.0)
        o = jnp.einsum('bhij,bhjv->bhiv', att, vi) + jnp.einsum('bhid,bhdv->bhiv', q_g, S)
        k_dec = ki * jnp.exp(b_last - b)
        S = jnp.exp(b_last)[:, :, 0, :, None] * S + jnp.einsum('bhjd,bhjv->bhdv', k_dec, vi)
        return S, o

    S, o = lax.scan(step, s0.astype(jnp.float32), (qc, kc, vc, gc))
    o = o.transpose(1, 0, 3, 2, 4).reshape(B, n * C, H, -1)[:, :L]
    return o, S


def gmlp_spatial(u, v, w_sp, b_sp):
    B, L, G, Cg = v.shape
    C = min(GMLP_CHUNK, L)
    n = L // C
    w = jnp.where(jnp.tril(jnp.ones((C, C), bool)), w_sp[:, :C, :C], 0.0)
    s = jnp.einsum('gij,bnjgc->bnigc', w, v.reshape(B, n, C, G, Cg))
    s = s + b_sp[:, :C].T[None, None, :, :, None]
    return u * s.reshape(B, L, G, Cg)


def even_mixer(x, s0, w_in, w_gate_up, b_gate, gla_norm_g, gmlp_ln_g, gmlp_ln_b, w_sp, b_sp, w_out):
    B, L, _ = x.shape
    h = jnp.einsum('bld,de->ble', x, w_in)
    q, k, v, g_low, r, gu, gv = jnp.split(h, EVEN_SPLIT_POINTS, axis=-1)
    q = q.reshape(B, L, GLA_HEADS, GLA_DK) * (GLA_DK ** -0.5)
    k = k.reshape(B, L, GLA_HEADS, GLA_DK)
    v = v.reshape(B, L, GLA_HEADS, GLA_DV)
    z = jnp.einsum('blr,re->ble', g_low, w_gate_up) + b_gate
    log_a = (jax.nn.log_sigmoid(z.astype(jnp.float32)) / GLA_TAU).reshape(B, L, GLA_HEADS, GLA_DK)
    o, s_new = gla_scan(q, k, v, log_a, s0)
    o = rms_norm(o, gla_norm_g) * jax.nn.silu(r.astype(jnp.float32)).reshape(B, L, GLA_HEADS, GLA_DV)
    o = o.reshape(B, L, GLA_V_W).astype(x.dtype)
    gu = jax.nn.gelu(gu).reshape(B, L, GMLP_GROUPS, GMLP_DG)
    gv = layer_norm(jax.nn.gelu(gv).reshape(B, L, GMLP_GROUPS, GMLP_DG), gmlp_ln_g, gmlp_ln_b)
    m = gmlp_spatial(gu, gv, w_sp, b_sp).reshape(B, L, GMLP_W).astype(x.dtype)
    y = jnp.einsum('ble,ed->bld', jnp.concatenate([o, m], -1), w_out)
    return y, s_new, gv


def mla_project(x, pos, w_in, q_norm_g, kv_norm_g, w_uq):
    h = jnp.einsum('bld,de->ble', x, w_in)
    cq, ckv, kpe = jnp.split(h, (Q_LORA, Q_LORA + KV_LORA), axis=-1)
    cq = rms_norm(cq, q_norm_g)
    ckv = rms_norm(ckv, kv_norm_g)
    q = jnp.einsum('blr,rhe->blhe', cq, w_uq)
    q_nope, q_pe = q[..., :QK_NOPE], q[..., QK_NOPE:]
    cos, sin = rope_tables(pos)
    q_pe = apply_rope(q_pe, cos[None, :, None, :], sin[None, :, None, :])
    kpe = apply_rope(kpe, cos[None], sin[None])
    return q_nope, q_pe, ckv, kpe


def mla_prompt_attend(q_nope, q_pe, ckv, kpe, w_uk, w_uv):
    B, L, H, _ = q_nope.shape
    k_nope = jnp.einsum('blr,rhe->blhe', ckv, w_uk)
    v = jnp.einsum('blr,rhe->blhe', ckv, w_uv)
    Qb = min(Q_BLOCK, L)
    nb = L // Qb
    kpos = jnp.arange(L)

    def block(i):
        qn = lax.dynamic_slice_in_dim(q_nope, i * Qb, Qb, axis=1)
        qp = lax.dynamic_slice_in_dim(q_pe, i * Qb, Qb, axis=1)
        s = jnp.einsum('bqhe,bkhe->bhqk', qn, k_nope) + jnp.einsum('bqhe,bke->bhqk', qp, kpe)
        s = s.astype(jnp.float32) * MLA_SCALE
        qpos = i * Qb + jnp.arange(Qb)
        s = jnp.where(kpos[None, :] <= qpos[:, None], s, NEG_INF)
        p = jax.nn.softmax(s, axis=-1).astype(v.dtype)
        return jnp.einsum('bhqk,bkhe->bqhe', p, v)

    o = lax.map(block, jnp.arange(nb))
    return o.transpose(1, 0, 2, 3, 4).reshape(B, L, H * V_HEAD)


def mla_sample_attend(q_nope, q_pe, ckv_new, kpe_new, cache_ckv, cache_kpe, layer, page_table, w_uk, w_uv):
    B, T, H, _ = q_nope.shape
    q_lat = jnp.einsum('bthe,rhe->bthr', q_nope, w_uk)
    new_mask = jnp.tril(jnp.ones((T, T), bool))

    def one_seq(args):
        ql, qp, cn, kn, pages = args
        past_c = cache_ckv[layer, pages].reshape(-1, KV_LORA)
        past_k = cache_kpe[layer, pages].reshape(-1, QK_ROPE)
        n_past = past_c.shape[0]
        s_past = jnp.einsum('thr,kr->htk', ql, past_c) + jnp.einsum('the,ke->htk', qp, past_k)
        s_new = jnp.einsum('thr,kr->htk', ql, cn) + jnp.einsum('the,ke->htk', qp, kn)
        s_new = jnp.where(new_mask, s_new.astype(jnp.float32), NEG_INF)
        s = jnp.concatenate([s_past.astype(jnp.float32), s_new], -1) * MLA_SCALE
        p = jax.nn.softmax(s, axis=-1).astype(ql.dtype)
        return (jnp.einsum('htk,kr->thr', p[..., :n_past], past_c)
                + jnp.einsum('htk,kr->thr', p[..., n_past:], cn))

    o_lat = lax.map(one_seq, (q_lat, q_pe, ckv_new, kpe_new, page_table))
    o = jnp.einsum('bthr,rhe->bthe', o_lat, w_uv)
    return o.reshape(B, T, H * V_HEAD)


def peer(x, w_q, keys, peer_u, peer_v, layer):
    B, L, D = x.shape
    T = B * L
    xt = x.reshape(T, D)
    q = jnp.einsum('td,de->te', xt, w_q).reshape(T, PEER_HEADS, 2, PEER_QDIM // 2)
    sc = jnp.einsum('thpc,hpnc->thpn', q, keys).astype(jnp.float32)
    s_top, i_top = lax.top_k(sc, PEER_TOPK)
    cand = s_top[:, :, 0, :, None] + s_top[:, :, 1, None, :]
    cand_idx = i_top[:, :, 0, :, None] * PEER_NKEYS + i_top[:, :, 1, None, :]
    s_best, j = lax.top_k(cand.reshape(T, PEER_HEADS, PEER_TOPK * PEER_TOPK), PEER_TOPK)
    e_idx = jnp.take_along_axis(cand_idx.reshape(T, PEER_HEADS, -1), j, axis=-1)
    gate = jax.nn.softmax(s_best, axis=-1)
    e_idx = e_idx.reshape(T, PEER_HEADS * PEER_TOPK)
    gate = gate.reshape(T, PEER_HEADS * PEER_TOPK)
    TB = PEER_TOK_BLOCK
    nb = -(-T // TB)
    pad = nb * TB - T
    xt_p = jnp.pad(xt, ((0, pad), (0, 0))).reshape(nb, TB, D)
    idx_p = jnp.pad(e_idx, ((0, pad), (0, 0))).reshape(nb, TB, -1)
    gate_p = jnp.pad(gate, ((0, pad), (0, 0))).reshape(nb, TB, -1)

    def block(args):
        xb, ib, gb = args
        h = jnp.einsum('td,tkd->tk', xb, peer_u[layer, ib])
        w = (gb * jax.nn.gelu(h.astype(jnp.float32))).astype(xb.dtype)
        return jnp.einsum('tk,tkd->td', w, peer_v[layer, ib])

    out = lax.map(block, (xt_p, idx_p, gate_p))
    return out.reshape(nb * TB, D)[:T].reshape(B, L, D)


def setup_inputs(seed: int = 0) -> dict:
    key = jax.random.key(seed)
    ks = iter(jax.random.split(key, 40))

    def nrm(shape, scale):
        return jax.random.normal(next(ks), shape, jnp.float32) * scale

    n_pages = PAST_LEN // PAGE_SIZE
    n_used = DEC_BATCH * n_pages
    n_pool = n_used + -(-n_used // 4)
    page_table = jax.random.permutation(next(ks), n_pool)[:n_used].reshape(DEC_BATCH, n_pages).astype(jnp.int32)
    col_scale = jnp.concatenate([jnp.full((w,), DN_BETA if j == 2 else 1.0, jnp.float32)
                                 for j, w in enumerate(EVEN_IN_WIDTHS)])
    return {
        'x_prompt': nrm((BATCH, SEQ, D_MODEL), 1.0),
        'x_sample': nrm((DEC_BATCH, DEC_SEQ, D_MODEL), 1.0),
        'state_gla': nrm((N_EVEN, DEC_BATCH, GLA_HEADS, GLA_DK, GLA_DV), 0.5),
        'cache_ckv': nrm((N_ODD, n_pool, PAGE_SIZE, KV_LORA), 1.0),
        'cache_kpe': nrm((N_ODD, n_pool, PAGE_SIZE, QK_ROPE), 1.0),
        'page_table': page_table,
        'w_in_even': nrm((N_EVEN, D_MODEL, EVEN_IN_W), D_MODEL ** -0.5) * col_scale,
        'w_gate_up': nrm((N_EVEN, GLA_GATE_RANK, GLA_QK_W), GLA_GATE_RANK ** -0.5),
        'b_gate': nrm((N_EVEN, GLA_QK_W), 0.5),
        'gla_norm_g': 1.0 + nrm((N_EVEN, GLA_DV), 0.05),
        'gmlp_ln_g': 1.0 + nrm((N_EVEN, GMLP_GROUPS, GMLP_DG), 0.05),
        'gmlp_ln_b': nrm((N_EVEN, GMLP_GROUPS, GMLP_DG), 0.05),
        'w_spatial': nrm((N_EVEN, GMLP_GROUPS, GMLP_CHUNK, GMLP_CHUNK), GMLP_CHUNK ** -0.5),
        'b_spatial': 1.0 + nrm((N_EVEN, GMLP_GROUPS, GMLP_CHUNK), 0.1),
        'w_out_even': nrm((N_EVEN, EVEN_OUT_W, D_MODEL), DN_BETA * EVEN_OUT_W ** -0.5),
        'w_in_odd': nrm((N_ODD, D_MODEL, ODD_IN_W), D_MODEL ** -0.5),
        'q_norm_g': 1.0 + nrm((N_ODD, Q_LORA), 0.05),
        'kv_norm_g': 1.0 + nrm((N_ODD, KV_LORA), 0.05),
        'w_uq': nrm((N_ODD, Q_LORA, MLA_HEADS, QK_NOPE + QK_ROPE), Q_LORA ** -0.5),
        'w_uk': nrm((N_ODD, KV_LORA, MLA_HEADS, QK_NOPE), KV_LORA ** -0.5),
        'w_uv': nrm((N_ODD, KV_LORA, MLA_HEADS, V_HEAD), DN_BETA * KV_LORA ** -0.5),
        'w_o_odd': nrm((N_ODD, MLA_HEADS * V_HEAD, D_MODEL), DN_BETA * (MLA_HEADS * V_HEAD) ** -0.5),
        'ln_mix_g': 1.0 + nrm((DEPTH, D_MODEL), 0.05),
        'ln_mix_b': nrm((DEPTH, D_MODEL), 0.05),
        'ln_ffn_g': 1.0 + nrm((DEPTH, D_MODEL), 0.05),
        'ln_ffn_b': nrm((DEPTH, D_MODEL), 0.05),
        'w_peer_q': nrm((DEPTH, D_MODEL, PEER_HEADS * PEER_QDIM), D_MODEL ** -0.5),
        'peer_keys': nrm((DEPTH, PEER_HEADS, 2, PEER_NKEYS, PEER_QDIM // 2), (PEER_QDIM // 2) ** -0.5),
        'peer_u': nrm((DEPTH, PEER_NEXPERTS, D_MODEL), D_MODEL ** -0.5),
        'peer_v': nrm((DEPTH, PEER_NEXPERTS, D_MODEL), DN_BETA * PEER_HEADS ** -0.5),
    }


def reference(x_prompt, x_sample, state_gla, cache_ckv, cache_kpe, page_table,
              w_in_even, w_gate_up, b_gate, gla_norm_g, gmlp_ln_g, gmlp_ln_b, w_spatial, b_spatial, w_out_even,
              w_in_odd, q_norm_g, kv_norm_g, w_uq, w_uk, w_uv, w_o_odd,
              ln_mix_g, ln_mix_b, ln_ffn_g, ln_ffn_b, w_peer_q, peer_keys, peer_u, peer_v):
    xp, xs = x_prompt, x_sample
    Lp, Ls = xp.shape[1], xs.shape[1]
    past_len = page_table.shape[1] * cache_ckv.shape[2]
    pos_p = jnp.arange(Lp, dtype=jnp.int32)
    pos_s = past_len + jnp.arange(Ls, dtype=jnp.int32)
    gla_p, gla_s, gmlp_s, ckv_p, kpe_p, ckv_s, kpe_s = [], [], [], [], [], [], []
    for l in range(DEPTH):
        i = l // 2
        if l % 2 == 0:
            ew = (w_in_even[i], w_gate_up[i], b_gate[i], gla_norm_g[i], gmlp_ln_g[i], gmlp_ln_b[i],
                  w_spatial[i], b_spatial[i], w_out_even[i])
            s0 = jnp.zeros((xp.shape[0], GLA_HEADS, GLA_DK, GLA_DV), jnp.float32)
            yp, sp, _ = even_mixer(xp, s0, *ew)
            ys, ss, vs = even_mixer(xs, state_gla[i], *ew)
            gla_p.append(sp)
            gla_s.append(ss)
            gmlp_s.append(vs)
        else:
            qn, qpe, cp, kp = mla_project(xp, pos_p, w_in_odd[i], q_norm_g[i], kv_norm_g[i], w_uq[i])
            yp = jnp.einsum('ble,ed->bld', mla_prompt_attend(qn, qpe, cp, kp, w_uk[i], w_uv[i]), w_o_odd[i])
            qn_s, qpe_s, cs, ksp = mla_project(xs, pos_s, w_in_odd[i], q_norm_g[i], kv_norm_g[i], w_uq[i])
            o_s = mla_sample_attend(qn_s, qpe_s, cs, ksp, cache_ckv, cache_kpe, i, page_table, w_uk[i], w_uv[i])
            ys = jnp.einsum('ble,ed->bld', o_s, w_o_odd[i])
            ckv_p.append(cp)
            kpe_p.append(kp)
            ckv_s.append(cs)
            kpe_s.append(ksp)
        xp = layer_norm(DN_ALPHA * xp + yp, ln_mix_g[l], ln_mix_b[l])
        xs = layer_norm(DN_ALPHA * xs + ys, ln_mix_g[l], ln_mix_b[l])
        xp = layer_norm(DN_ALPHA * xp + peer(xp, w_peer_q[l], peer_keys[l], peer_u, peer_v, l), ln_ffn_g[l], ln_ffn_b[l])
        xs = layer_norm(DN_ALPHA * xs + peer(xs, w_peer_q[l], peer_keys[l], peer_u, peer_v, l), ln_ffn_g[l], ln_ffn_b[l])
    return (xp, xs, jnp.stack(gla_p), jnp.stack(gla_s), jnp.stack(gmlp_s),
            jnp.stack(ckv_p), jnp.stack(kpe_p), jnp.stack(ckv_s), jnp.stack(kpe_s))
```

```python
import functools

import jax
import jax.numpy as jnp
from jax import lax
from jax.experimental import pallas as pl
from jax.experimental.pallas import tpu as pltpu

F32 = jnp.float32
BF16 = jnp.bfloat16

GLA_HEADS = 4
GLA_DK = 128
GLA_DV = 256
GLA_GATE_RANK = 16
GLA_TAU = 16.0
GLA_CHUNK = 64
GMLP_GROUPS = 4
GMLP_DG = 256
GMLP_CHUNK = 128
MLA_HEADS = 16
Q_LORA = 512
KV_LORA = 512
QK_NOPE = 128
QK_ROPE = 64
V_HEAD = 128
ROPE_THETA = 10000.0
MLA_SCALE = (QK_NOPE + QK_ROPE) ** -0.5
PEER_HEADS = 8
PEER_NKEYS = 128
PEER_TOPK = 16
LN_EPS = 1e-5
RMS_EPS = 1e-6
NEG_INF = -1e30

LANES = 128
SUBLANES = 8
VMEM_LIMIT_BYTES = 56 * 1024 * 1024

GLOW_PAD = 256


def _cparams(sem):
    return pltpu.CompilerParams(dimension_semantics=sem, vmem_limit_bytes=VMEM_LIMIT_BYTES)


def _pick(n, prefs):
    for p in prefs:
        if n % p == 0:
            return p
    raise ValueError(f"no tile in {prefs} divides {n}")


def _gelu(x):
    return 0.5 * x * (1.0 + jnp.tanh(0.7978845608028654 * (x + 0.044715 * (x * x * x))))


def _dot(a, b, **kw):
    return jnp.dot(a, b, preferred_element_type=F32, **kw)


def _dot_nt(a, b, **kw):
    return lax.dot_general(a, b, (((1,), (1,)), ((), ())), preferred_element_type=F32, **kw)


def _dot_tn(a, b, **kw):
    return lax.dot_general(a, b, (((0,), (0,)), ((), ())), preferred_element_type=F32, **kw)


def _mm_kernel(x_ref, w_ref, o_ref, *, scale):
    acc = _dot(x_ref[...], w_ref[...])
    if scale != 1.0:
        acc = acc * scale
    o_ref[...] = acc.astype(o_ref.dtype)


def mm(x, w, out_dtype, *, rows=None, row_start=0, scale=1.0, tn_prefs=(1792, 1024, 768, 512, 384, 256, 128)):
    M, K = x.shape
    N = w.shape[1]
    rows = M - row_start if rows is None else rows
    tm = _pick(rows, (1024, 512, 256, 128))
    assert row_start % tm == 0
    tn = _pick(N, tn_prefs)
    r0 = row_start // tm
    return pl.pallas_call(
        functools.partial(_mm_kernel, scale=scale),
        grid=(rows // tm, N // tn),
        in_specs=[pl.BlockSpec((tm, K), lambda i, j: (i + r0, 0)),
                  pl.BlockSpec((K, tn), lambda i, j: (0, j))],
        out_specs=pl.BlockSpec((tm, tn), lambda i, j: (i, j)),
        out_shape=jax.ShapeDtypeStruct((rows, N), out_dtype),
        compiler_params=_cparams(("parallel", "parallel")),
    )(x, w)


def _mm2_kernel(x1_ref, w1_ref, x2_ref, w2_ref, o_ref):
    o_ref[...] = _dot(x1_ref[...], w1_ref[...]) + _dot(x2_ref[...], w2_ref[...])


def mm2(x1, w1, x2, w2):
    M, K1 = x1.shape
    K2 = x2.shape[1]
    N = w1.shape[1]
    tm = _pick(M, (1024, 512, 256, 128))
    tn = _pick(N, (1024, 512, 256, 128))
    return pl.pallas_call(
        _mm2_kernel,
        grid=(M // tm, N // tn),
        in_specs=[pl.BlockSpec((tm, K1), lambda i, j: (i, 0)),
                  pl.BlockSpec((K1, tn), lambda i, j: (0, j)),
                  pl.BlockSpec((tm, K2), lambda i, j: (i, 0)),
                  pl.BlockSpec((K2, tn), lambda i, j: (0, j))],
        out_specs=pl.BlockSpec((tm, tn), lambda i, j: (i, j)),
        out_shape=jax.ShapeDtypeStruct((M, N), F32),
        compiler_params=_cparams(("parallel", "parallel")),
    )(x1, w1, x2, w2)


def _headmm_kernel(x_ref, w_ref, o_ref, *, transpose_w):
    if transpose_w:
        o_ref[...] = _dot_nt(x_ref[...], w_ref[...]).astype(o_ref.dtype)
    else:
        o_ref[...] = _dot(x_ref[...], w_ref[...]).astype(o_ref.dtype)


def head_mm(x, w, heads, kin, kout, *, transpose_w, row_start=0, rows=None):
    M = x.shape[0]
    rows = M - row_start if rows is None else rows
    tm = _pick(rows, (1024, 512, 256, 128))
    assert row_start % tm == 0
    r0 = row_start // tm
    wblock = (kout, kin) if transpose_w else (kin, kout)
    return pl.pallas_call(
        functools.partial(_headmm_kernel, transpose_w=transpose_w),
        grid=(rows // tm, heads),
        in_specs=[pl.BlockSpec((tm, kin), lambda i, h: (i + r0, h)),
                  pl.BlockSpec(wblock, lambda i, h: (0, h))],
        out_specs=pl.BlockSpec((tm, kout), lambda i, h: (i, h)),
        out_shape=jax.ShapeDtypeStruct((rows, heads * kout), BF16),
        compiler_params=_cparams(("parallel", "parallel")),
    )(x, w)


def _ln_kernel(x_ref, y_ref, g_ref, b_ref, of_ref, ob_ref, *, alpha):
    z = alpha * x_ref[...] + y_ref[...]
    mu = jnp.mean(z, axis=-1, keepdims=True)
    zc = z - mu
    var = jnp.mean(zc * zc, axis=-1, keepdims=True)
    out = zc * lax.rsqrt(var + LN_EPS) * g_ref[...] + b_ref[...]
    of_ref[...] = out
    ob_ref[...] = out.astype(BF16)


def residual_ln(x, y, g, b, alpha):
    T, D = x.shape
    tm = _pick(T, (256, 128))
    row = pl.BlockSpec((tm, D), lambda i: (i, 0))
    vec = pl.BlockSpec((1, D), lambda i: (0, 0))
    return pl.pallas_call(
        functools.partial(_ln_kernel, alpha=alpha),
        grid=(T // tm,),
        in_specs=[row, row, vec, vec],
        out_specs=[row, row],
        out_shape=[jax.ShapeDtypeStruct((T, D), F32), jax.ShapeDtypeStruct((T, D), BF16)],
        compiler_params=_cparams(("parallel",)),
    )(x, y, g.reshape(1, D), b.reshape(1, D))


def _gla_kernel(q_ref, k_ref, v_ref, gl_ref, r_ref, wg_ref, bg_ref, gn_ref, s0_ref,
                o_ref, s_ref, state, *, chunk, n_chunks, mxu_dtype):
    rb = pl.program_id(2)

    @pl.when(rb == 0)
    def _():
        state[...] = s0_ref[0, 0]

    C = chunk
    row = lax.broadcasted_iota(jnp.int32, (C, C), 0)
    col = lax.broadcasted_iota(jnp.int32, (C, C), 1)
    causal = row >= col
    tri = causal.astype(F32)
    ones_cv = jnp.ones((C, GLA_DV), F32)
    hi = lax.Precision.HIGHEST
    md = mxu_dtype

    for c in range(n_chunks):
        sl = slice(c * C, (c + 1) * C)
        qi = q_ref[sl, :] * (GLA_DK ** -0.5)
        ki = k_ref[sl, :]
        vi = v_ref[sl, :]
        z = _dot(gl_ref[sl, :].astype(BF16), wg_ref[...]) + bg_ref[...]
        gi = (jnp.minimum(z, 0.0) - jnp.log(1.0 + jnp.exp(-jnp.abs(z)))) * (1.0 / GLA_TAU)
        b = _dot(tri, gi, precision=hi)
        b_last = b[C - 1:C, :]
        q_g = qi * jnp.exp(b)
        k_g = ki * jnp.exp(-b)
        att = jnp.where(causal, _dot_nt(q_g.astype(md), k_g.astype(md)), 0.0)
        s_prev = state[...]
        o = _dot(att.astype(md), vi.astype(md)) + _dot(q_g.astype(md), s_prev.astype(md))
        k_dec = ki * jnp.exp(b_last - b)
        decay = _dot_tn(gi, ones_cv, precision=hi)
        state[...] = jnp.exp(decay) * s_prev + _dot_tn(k_dec.astype(md), vi.astype(md))
        ms = jnp.mean(o * o, axis=-1, keepdims=True)
        on = o * lax.rsqrt(ms + RMS_EPS) * gn_ref[...]
        rr = r_ref[sl, :]
        o_ref[sl, :] = (on * (rr * jax.nn.sigmoid(rr))).astype(o_ref.dtype)

    @pl.when(rb == pl.num_programs(2) - 1)
    def _():
        s_ref[0, 0] = state[...]


def gla(hm, wg, bg, gn, s0, *, row_start, batch, seq):
    C = min(GLA_CHUNK, seq)
    assert seq % C == 0
    n_chunks = _pick(seq // C, (4, 2, 1))
    R = C * n_chunks
    nrb = seq // R
    assert row_start % R == 0
    r0 = row_start // R
    H = GLA_HEADS
    glow_blk = (2 * H * GLA_DK + 2 * H * GLA_DV + 2 * GMLP_GROUPS * GMLP_DG) // LANES
    rowmap = lambda off: (lambda b, h, rb: (r0 + b * nrb + rb, off + h))
    mxu_dtype = BF16 if C >= 16 else F32
    out_dtype = BF16 if R >= 16 else F32
    kernel = functools.partial(_gla_kernel, chunk=C, n_chunks=n_chunks, mxu_dtype=mxu_dtype)
    return pl.pallas_call(
        kernel,
        grid=(batch, H, nrb),
        in_specs=[
            pl.BlockSpec((R, GLA_DK), rowmap(0)),
            pl.BlockSpec((R, GLA_DK), rowmap(H)),
            pl.BlockSpec((R, GLA_DV), rowmap(H)),
            pl.BlockSpec((R, LANES), lambda b, h, rb: (r0 + b * nrb + rb, glow_blk)),
            pl.BlockSpec((R, GLA_DV), rowmap(2 * H)),
            pl.BlockSpec((LANES, GLA_DK), lambda b, h, rb: (0, h)),
            pl.BlockSpec((1, GLA_DK), lambda b, h, rb: (0, h)),
            pl.BlockSpec((1, GLA_DV), lambda b, h, rb: (0, 0)),
            pl.BlockSpec((1, 1, GLA_DK, GLA_DV), lambda b, h, rb: (b, h, 0, 0)),
        ],
        out_specs=[
            pl.BlockSpec((R, GLA_DV), lambda b, h, rb: (b * nrb + rb, h)),
            pl.BlockSpec((1, 1, GLA_DK, GLA_DV), lambda b, h, rb: (b, h, 0, 0)),
        ],
        out_shape=[jax.ShapeDtypeStruct((batch * seq, H * GLA_DV), out_dtype),
                   jax.ShapeDtypeStruct((batch, H, GLA_DK, GLA_DV), F32)],
        scratch_shapes=[pltpu.VMEM((GLA_DK, GLA_DV), F32)],
        compiler_params=_cparams(("parallel", "parallel", "arbitrary")),
    )(hm, hm, hm, hm, hm, wg, bg, gn, s0)


def _gmlp_kernel(gu_ref, gv_ref, lng_ref, lnb_ref, w_ref, b_ref, m_ref, *gvo_ref):
    u = _gelu(gu_ref[...])
    v = _gelu(gv_ref[...])
    mu = jnp.mean(v, axis=-1, keepdims=True)
    vc = v - mu
    var = jnp.mean(vc * vc, axis=-1, keepdims=True)
    vn = vc * lax.rsqrt(var + LN_EPS) * lng_ref[0] + lnb_ref[0]
    s = _dot(w_ref[0], vn.astype(BF16)) + b_ref[0]
    m_ref[...] = (u * s).astype(m_ref.dtype)
    if gvo_ref:
        gvo_ref[0][...] = vn


def gmlp(hm, lng, lnb, w_eff, b_eff, *, row_start, rows, emit_v):
    G = GMLP_GROUPS
    R = GMLP_CHUNK
    assert rows % R == 0 and row_start % R == 0
    r0 = row_start // R
    ublk = (2 * GLA_HEADS * GLA_DK + 2 * GLA_HEADS * GLA_DV) // GMLP_DG
    vblk = ublk + G
    out_block = pl.BlockSpec((R, GMLP_DG), lambda i, g: (i, g))
    out_specs = [out_block]
    out_shape = [jax.ShapeDtypeStruct((rows, G * GMLP_DG), BF16)]
    if emit_v:
        out_specs.append(out_block)
        out_shape.append(jax.ShapeDtypeStruct((rows, G * GMLP_DG), F32))
    return pl.pallas_call(
        _gmlp_kernel,
        grid=(rows // R, G),
        in_specs=[
            pl.BlockSpec((R, GMLP_DG), lambda i, g: (r0 + i, ublk + g)),
            pl.BlockSpec((R, GMLP_DG), lambda i, g: (r0 + i, vblk + g)),
            pl.BlockSpec((1, 1, GMLP_DG), lambda i, g: (g, 0, 0)),
            pl.BlockSpec((1, 1, GMLP_DG), lambda i, g: (g, 0, 0)),
            pl.BlockSpec((1, R, R), lambda i, g: (g, 0, 0)),
            pl.BlockSpec((1, R, 1), lambda i, g: (g, 0, 0)),
        ],
        out_specs=out_specs,
        out_shape=out_shape,
        compiler_params=_cparams(("parallel", "parallel")),
    )(hm, hm, lng.reshape(G, 1, GMLP_DG), lnb.reshape(G, 1, GMLP_DG), w_eff, b_eff.reshape(G, R, 1))


def _mla_post_kernel(h_ref, cs_ref, qg_ref, kg_ref, cq_ref, ckv_ref, ckvb_ref, kpe_ref, kpeb_ref):
    cq = h_ref[:, :Q_LORA]
    ckv = h_ref[:, Q_LORA:Q_LORA + KV_LORA]
    cqn = cq * lax.rsqrt(jnp.mean(cq * cq, axis=-1, keepdims=True) + RMS_EPS) * qg_ref[...]
    ckvn = ckv * lax.rsqrt(jnp.mean(ckv * ckv, axis=-1, keepdims=True) + RMS_EPS) * kg_ref[...]
    cq_ref[...] = cqn.astype(BF16)
    ckv_ref[...] = ckvn
    ckvb_ref[...] = ckvn.astype(BF16)
    t = h_ref[:, Q_LORA + KV_LORA:] * cs_ref[...]
    rot = t + pltpu.roll(t, QK_ROPE, 1)
    lane = lax.broadcasted_iota(jnp.int32, rot.shape, 1)
    rot = jnp.where(lane < QK_ROPE, rot, 0.0)
    kpe_ref[...] = rot
    kpeb_ref[...] = rot.astype(BF16)


def mla_post(h, cs, q_norm_g, kv_norm_g):
    T = h.shape[0]
    tm = _pick(T, (512, 256, 128))
    W = h.shape[1]
    return pl.pallas_call(
        _mla_post_kernel,
        grid=(T // tm,),
        in_specs=[pl.BlockSpec((tm, W), lambda i: (i, 0)),
                  pl.BlockSpec((tm, LANES), lambda i: (i, 0)),
                  pl.BlockSpec((1, Q_LORA), lambda i: (0, 0)),
                  pl.BlockSpec((1, KV_LORA), lambda i: (0, 0))],
        out_specs=[pl.BlockSpec((tm, Q_LORA), lambda i: (i, 0)),
                   pl.BlockSpec((tm, KV_LORA), lambda i: (i, 0)),
                   pl.BlockSpec((tm, KV_LORA), lambda i: (i, 0)),
                   pl.BlockSpec((tm, LANES), lambda i: (i, 0)),
                   pl.BlockSpec((tm, LANES), lambda i: (i, 0))],
        out_shape=[jax.ShapeDtypeStruct((T, Q_LORA), BF16),
                   jax.ShapeDtypeStruct((T, KV_LORA), F32),
                   jax.ShapeDtypeStruct((T, KV_LORA), BF16),
                   jax.ShapeDtypeStruct((T, LANES), F32),
                   jax.ShapeDtypeStruct((T, LANES), BF16)],
        compiler_params=_cparams(("parallel",)),
    )(h, cs, q_norm_g.reshape(1, Q_LORA), kv_norm_g.reshape(1, KV_LORA))


def _qpe_kernel(cq_ref, w_ref, wsw_ref, cs_ref, o_ref):
    a = _dot(cq_ref[...], w_ref[...])
    b = _dot(cq_ref[...], wsw_ref[...])
    cos = cs_ref[:, :LANES]
    sin = cs_ref[:, LANES:]
    for h in range(MLA_HEADS):
        sl = slice(h * LANES, (h + 1) * LANES)
        o_ref[:, sl] = ((a[:, sl] * cos + b[:, sl] * sin) * MLA_SCALE).astype(o_ref.dtype)


def q_rope(cq, w_pe, w_pe_sw, cs2):
    T, K = cq.shape
    N = w_pe.shape[1]
    tm = _pick(T, (512, 256, 128))
    return pl.pallas_call(
        _qpe_kernel,
        grid=(T // tm,),
        in_specs=[pl.BlockSpec((tm, K), lambda i: (i, 0)),
                  pl.BlockSpec((K, N), lambda i: (0, 0)),
                  pl.BlockSpec((K, N), lambda i: (0, 0)),
                  pl.BlockSpec((tm, 2 * LANES), lambda i: (i, 0))],
        out_specs=pl.BlockSpec((tm, N), lambda i: (i, 0)),
        out_shape=jax.ShapeDtypeStruct((T, N), BF16),
        compiler_params=_cparams(("parallel",)),
    )(cq, w_pe, w_pe_sw, cs2)


def _flash_kernel(qn_ref, qp_ref, kn_ref, kp_ref, v_ref, o_ref, m_scr, l_scr, acc_scr, *, tq, tk):
    qi = pl.program_id(2)
    ki = pl.program_id(3)

    @pl.when(ki == 0)
    def _():
        m_scr[...] = jnp.full_like(m_scr, NEG_INF)
        l_scr[...] = jnp.zeros_like(l_scr)
        acc_scr[...] = jnp.zeros_like(acc_scr)

    @pl.when(ki * tk <= qi * tq + (tq - 1))
    def _():
        q = jnp.concatenate([qn_ref[...], qp_ref[...]], axis=1)
        k = jnp.concatenate([kn_ref[...], kp_ref[...]], axis=1)
        s = _dot_nt(q, k)
        qpos = qi * tq + lax.broadcasted_iota(jnp.int32, s.shape, 0)
        kpos = ki * tk + lax.broadcasted_iota(jnp.int32, s.shape, 1)
        s = jnp.where(kpos <= qpos, s, NEG_INF)
        m_prev = m_scr[...]
        m_new = jnp.maximum(m_prev, jnp.max(s, axis=1, keepdims=True))
        alpha = jnp.exp(m_prev - m_new)
        p = jnp.exp(s - m_new)
        l_scr[...] = alpha * l_scr[...] + jnp.sum(p, axis=1, keepdims=True)
        acc_scr[...] = alpha * acc_scr[...] + _dot(p.astype(BF16), v_ref[...])
        m_scr[...] = m_new

    @pl.when(ki == pl.num_programs(3) - 1)
    def _():
        o_ref[...] = (acc_scr[...] / l_scr[...]).astype(o_ref.dtype)


def mla_prompt_attention(qn, qp, kn, kp, v, *, batch, seq):
    tq = _pick(seq, (512, 256, 128))
    tk = tq
    nq = seq // tq
    nk = seq // tk
    H = MLA_HEADS

    def kmap(b, h, qi, ki):
        return (b * nk + jnp.minimum(ki, qi), h)

    return pl.pallas_call(
        functools.partial(_flash_kernel, tq=tq, tk=tk),
        grid=(batch, H, nq, nk),
        in_specs=[pl.BlockSpec((tq, LANES), lambda b, h, qi, ki: (b * nq + qi, h)),
                  pl.BlockSpec((tq, LANES), lambda b, h, qi, ki: (b * nq + qi, h)),
                  pl.BlockSpec((tk, LANES), kmap),
                  pl.BlockSpec((tk, LANES), lambda b, h, qi, ki: (b * nk + jnp.minimum(ki, qi), 0)),
                  pl.BlockSpec((tk, LANES), kmap)],
        out_specs=pl.BlockSpec((tq, LANES), lambda b, h, qi, ki: (b * nq + qi, h)),
        out_shape=jax.ShapeDtypeStruct((batch * seq, H * V_HEAD), BF16),
        scratch_shapes=[pltpu.VMEM((tq, 1), F32), pltpu.VMEM((tq, 1), F32), pltpu.VMEM((tq, V_HEAD), F32)],
        compiler_params=_cparams(("parallel", "parallel", "parallel", "arbitrary")),
    )(qn, qp, kn, kp, v)


def _paged_kernel(pt_ref, ql_ref, qp_ref, cn_ref, kn_ref, *rest, pages_per_step, dec_seq):
    P = pages_per_step
    ckv_refs = rest[:P]
    kpe_refs = rest[P:2 * P]
    o_ref = rest[2 * P]
    m_scr, l_scr, acc_scr = rest[2 * P + 1:]
    j = pl.program_id(1)
    rows = dec_seq * MLA_HEADS

    @pl.when(j == 0)
    def _():
        m_scr[...] = jnp.full_like(m_scr, NEG_INF)
        l_scr[...] = jnp.zeros_like(l_scr)
        acc_scr[...] = jnp.zeros_like(acc_scr)

    ql = ql_ref[...].reshape(rows, KV_LORA)
    qp = qp_ref[...].reshape(rows, LANES)
    qp_rope = qp[:, :QK_ROPE]

    kc = jnp.concatenate([r[0].astype(BF16) for r in ckv_refs], axis=0)
    kk = jnp.concatenate([r[0].astype(BF16) for r in kpe_refs], axis=0)
    s = _dot_nt(ql, kc) + _dot_nt(qp_rope, kk)
    m_prev = m_scr[...]
    m_new = jnp.maximum(m_prev, jnp.max(s, axis=1, keepdims=True))
    alpha = jnp.exp(m_prev - m_new)
    p = jnp.exp(s - m_new)
    l_scr[...] = alpha * l_scr[...] + jnp.sum(p, axis=1, keepdims=True)
    acc_scr[...] = alpha * acc_scr[...] + _dot(p.astype(BF16), kc)
    m_scr[...] = m_new

    @pl.when(j == pl.num_programs(1) - 1)
    def _():
        cn = cn_ref[...]
        kn = kn_ref[...]
        sn = _dot_nt(ql.astype(F32), cn) + _dot_nt(qp.astype(F32), kn)
        t_row = lax.broadcasted_iota(jnp.int32, sn.shape, 0) // MLA_HEADS
        t_col = lax.broadcasted_iota(jnp.int32, sn.shape, 1)
        sn = jnp.where(t_col <= t_row, sn, NEG_INF)
        m_prev = m_scr[...]
        m_new = jnp.maximum(m_prev, jnp.max(sn, axis=1, keepdims=True))
        alpha = jnp.exp(m_prev - m_new)
        pn = jnp.exp(sn - m_new)
        l = alpha * l_scr[...] + jnp.sum(pn, axis=1, keepdims=True)
        acc = alpha * acc_scr[...] + _dot(pn, cn)
        o_ref[...] = (acc / l).reshape(dec_seq, MLA_HEADS, KV_LORA).astype(o_ref.dtype)


def mla_sample_attention(page_table, q_lat, qp3, ckv_f32, kpe_f32, cache_ckv, cache_kpe, page_base, *, row_start, dec_seq):
    B, n_pages = page_table.shape
    page = cache_ckv.shape[1]
    P = _pick(n_pages, (8, 4, 2, 1))
    nsteps = n_pages // P
    assert row_start % dec_seq == 0
    r0 = row_start // dec_seq
    H = MLA_HEADS

    def page_map(p):
        return lambda b, j, pt: (page_base + pt[b * n_pages + j * P + p], 0, 0)

    in_specs = [
        pl.BlockSpec((dec_seq, H, KV_LORA), lambda b, j, pt: (b, 0, 0)),
        pl.BlockSpec((dec_seq, H, LANES), lambda b, j, pt: (r0 + b, 0, 0)),
        pl.BlockSpec((dec_seq, KV_LORA), lambda b, j, pt: (r0 + b, 0)),
        pl.BlockSpec((dec_seq, LANES), lambda b, j, pt: (r0 + b, 0)),
    ]
    in_specs += [pl.BlockSpec((1, page, KV_LORA), page_map(p)) for p in range(P)]
    in_specs += [pl.BlockSpec((1, page, QK_ROPE), page_map(p)) for p in range(P)]
    rows = dec_seq * H
    grid_spec = pltpu.PrefetchScalarGridSpec(
        num_scalar_prefetch=1,
        grid=(B, nsteps),
        in_specs=in_specs,
        out_specs=pl.BlockSpec((dec_seq, H, KV_LORA), lambda b, j, pt: (b, 0, 0)),
        scratch_shapes=[pltpu.VMEM((rows, 1), F32), pltpu.VMEM((rows, 1), F32), pltpu.VMEM((rows, KV_LORA), F32)],
    )
    return pl.pallas_call(
        functools.partial(_paged_kernel, pages_per_step=P, dec_seq=dec_seq),
        grid_spec=grid_spec,
        out_shape=jax.ShapeDtypeStruct((B * dec_seq, H, KV_LORA), BF16),
        compiler_params=_cparams(("parallel", "arbitrary")),
    )(page_table.reshape(-1), q_lat, qp3, ckv_f32, kpe_f32, *([cache_ckv] * P), *([cache_kpe] * P))


def _top_values(s, k):
    out = []
    for _ in range(k):
        m = jnp.max(s, axis=0, keepdims=True)
        out.append(m)
        s = jnp.where(s == m, -jnp.inf, s)
    return out


def _peer_topk_kernel(q_ref, keys_ref, s1_ref, c1_ref, s2_ref, e2_ref, tau_ref):
    K = PEER_TOPK
    for h in range(PEER_HEADS):
        qa = q_ref[:, (2 * h) * LANES:(2 * h + 1) * LANES]
        qb = q_ref[:, (2 * h + 1) * LANES:(2 * h + 2) * LANES]
        sa = _dot_nt(keys_ref[2 * h], qa)
        sb = _dot_nt(keys_ref[2 * h + 1], qb)
        ta = _top_values(sa, K)
        tb = _top_values(sb, K)
        tb_all = jnp.concatenate(tb, axis=0)
        cand = jnp.concatenate([ta[a] + tb_all for a in range(K)], axis=0)
        best = _top_values(cand, K)
        z = jnp.zeros_like(best[0])
        for r in range(K):
            z = z + jnp.exp(best[r] - best[0])
        s1_ref[h] = sa
        s2_ref[h] = sb
        c1_ref[h] = jnp.exp(sa - ta[0]) / z
        e2_ref[h] = jnp.exp(sb - tb[0])
        tau_ref[h:h + 1, :] = best[K - 1]


def peer_topk(q, keys):
    T = q.shape[0]
    TT = _pick(T, (512, 256, 128))
    blk = pl.BlockSpec((PEER_HEADS, PEER_NKEYS, TT), lambda i: (0, 0, i))
    sds = jax.ShapeDtypeStruct((PEER_HEADS, PEER_NKEYS, T), F32)
    return pl.pallas_call(
        _peer_topk_kernel,
        grid=(T // TT,),
        in_specs=[pl.BlockSpec((TT, q.shape[1]), lambda i: (i, 0)),
                  pl.BlockSpec(keys.shape, lambda i: (0, 0, 0))],
        out_specs=[blk, blk, blk, blk, pl.BlockSpec((PEER_HEADS, TT), lambda i: (0, i))],
        out_shape=[sds, sds, sds, sds, jax.ShapeDtypeStruct((PEER_HEADS, T), F32)],
        compiler_params=_cparams(("parallel",)),
    )(q, keys)


def _peer_dense_kernel(x_ref, u_ref, vt_ref, s1_ref, c1_ref, s2_ref, e2_ref, tau_ref, o_ref, acc_scr, a_scr,
                       *, groups_per_step):
    j = pl.program_id(1)

    @pl.when(j == 0)
    def _():
        acc_scr[...] = jnp.zeros_like(acc_scr)

    ht = _dot_nt(u_ref[0], x_ref[...])
    row0 = (j * groups_per_step) % SUBLANES
    for g in range(groups_per_step):
        gate = None
        for h in range(PEER_HEADS):
            s1 = s1_ref[h, pl.ds(row0 + g, 1), :]
            c1 = c1_ref[h, pl.ds(row0 + g, 1), :]
            total = s2_ref[h] + s1
            term = jnp.where(total >= tau_ref[h:h + 1, :], e2_ref[h] * c1, 0.0)
            gate = term if gate is None else gate + term
        sl = slice(g * PEER_NKEYS, (g + 1) * PEER_NKEYS)
        a_scr[sl, :] = (_gelu(ht[sl, :]) * gate).astype(BF16)
    acc_scr[...] += _dot(vt_ref[0], a_scr[...])

    @pl.when(j == pl.num_programs(1) - 1)
    def _():
        o_ref[...] = acc_scr[...].T


def peer_dense(x, u, vt, layer, s1, c1, s2, e2, tau):
    T, D = x.shape
    E = u.shape[1]
    TT = _pick(T, (512, 256, 128))
    TE = 512
    gps = TE // PEER_NKEYS
    assert SUBLANES % gps == 0 and E % TE == 0
    full = pl.BlockSpec((PEER_HEADS, PEER_NKEYS, TT), lambda i, j: (0, 0, i))
    rows8 = pl.BlockSpec((PEER_HEADS, SUBLANES, TT), lambda i, j: (0, (j * gps) // SUBLANES, i))
    return pl.pallas_call(
        functools.partial(_peer_dense_kernel, groups_per_step=gps),
        grid=(T // TT, E // TE),
        in_specs=[pl.BlockSpec((TT, D), lambda i, j: (i, 0)),
                  pl.BlockSpec((1, TE, D), lambda i, j: (layer, j, 0)),
                  pl.BlockSpec((1, D, TE), lambda i, j: (layer, 0, j)),
                  rows8, rows8, full, full,
                  pl.BlockSpec((PEER_HEADS, TT), lambda i, j: (0, i))],
        out_specs=pl.BlockSpec((TT, D), lambda i, j: (i, 0)),
        out_shape=jax.ShapeDtypeStruct((T, D), F32),
        scratch_shapes=[pltpu.VMEM((D, TT), F32), pltpu.VMEM((TE, TT), BF16)],
        compiler_params=_cparams(("parallel", "arbitrary")),
    )(x, u, vt, s1, c1, s2, e2, tau)


def _rope_swap_cols(w):
    half = QK_ROPE // 2
    return jnp.concatenate([-w[..., half:], w[..., :half]], axis=-1)


def _prep_even(w_in, w_gate_up, w_out):
    qk = 2 * GLA_HEADS * GLA_DK
    vw = GLA_HEADS * GLA_DV
    g0 = qk + vw
    g1 = g0 + GLA_GATE_RANK
    K = w_in.shape[0]
    w_perm = jnp.concatenate(
        [w_in[:, :g0], w_in[:, g1:], w_in[:, g0:g1], jnp.zeros((K, GLOW_PAD - GLA_GATE_RANK), w_in.dtype)], axis=1)
    wg = jnp.concatenate([w_gate_up, jnp.zeros((LANES - GLA_GATE_RANK, w_gate_up.shape[1]), w_gate_up.dtype)], axis=0)
    return w_perm.astype(BF16), wg.astype(BF16), w_out[:vw].astype(BF16), w_out[vw:].astype(BF16)


def _prep_gmlp_spatial(w_sp, b_sp, seq):
    C = min(GMLP_CHUNK, seq)
    assert seq % C == 0 and GMLP_CHUNK % C == 0
    w = jnp.where(jnp.tril(jnp.ones((C, C), bool)), w_sp[:, :C, :C], 0.0)
    b = b_sp[:, :C]
    rep = GMLP_CHUNK // C
    if rep > 1:
        eye = jnp.eye(rep, dtype=w.dtype)
        w = jnp.einsum('ab,gij->gaibj', eye, w).reshape(w.shape[0], GMLP_CHUNK, GMLP_CHUNK)
        b = jnp.tile(b, (1, rep))
    return w.astype(BF16), b


def _prep_odd(w_in, w_uq, w_uk, w_uv, w_o):
    kpe_w = w_in[:, Q_LORA + KV_LORA:]
    w_in_p = jnp.concatenate([w_in, _rope_swap_cols(kpe_w)], axis=1).astype(BF16)
    w_n = w_uq[:, :, :QK_NOPE].reshape(Q_LORA, MLA_HEADS * QK_NOPE).astype(BF16)
    pe = w_uq[:, :, QK_NOPE:]
    zpad = jnp.zeros((Q_LORA, MLA_HEADS, LANES - QK_ROPE), w_uq.dtype)
    w_pe = jnp.concatenate([pe, zpad], axis=-1).reshape(Q_LORA, MLA_HEADS * LANES).astype(BF16)
    w_pe_sw = jnp.concatenate([_rope_swap_cols(pe), zpad], axis=-1).reshape(Q_LORA, MLA_HEADS * LANES).astype(BF16)
    w_uk2 = w_uk.reshape(KV_LORA, MLA_HEADS * QK_NOPE).astype(BF16)
    w_uv2 = w_uv.reshape(KV_LORA, MLA_HEADS * V_HEAD).astype(BF16)
    return w_in_p, w_n, w_pe, w_pe_sw, w_uk2, w_uv2, w_o.astype(BF16)


def _rope_tables(pos):
    inv = ROPE_THETA ** (-jnp.arange(0, QK_ROPE, 2, dtype=F32) / QK_ROPE)
    ang = pos.astype(F32)[:, None] * inv[None, :]
    cos, sin = jnp.cos(ang), jnp.sin(ang)
    z = jnp.zeros((pos.shape[0], LANES - QK_ROPE), F32)
    cs_k = jnp.concatenate([cos, cos, sin, sin], axis=1)
    cs_q = jnp.concatenate([cos, cos, z, sin, sin, z], axis=1)
    return cs_k, cs_q


def kernel(x_prompt, x_sample, state_gla, cache_ckv, cache_kpe, page_table, w_in_even, w_gate_up, b_gate, gla_norm_g, gmlp_ln_g, gmlp_ln_b, w_spatial, b_spatial, w_out_even, w_in_odd, q_norm_g, kv_norm_g, w_uq, w_uk, w_uv, w_o_odd, ln_mix_g, ln_mix_b, ln_ffn_g, ln_ffn_b, w_peer_q, peer_keys, peer_u, peer_v):
    B, L, D = x_prompt.shape
    Bs, Ls, _ = x_sample.shape
    Tp, Ts = B * L, Bs * Ls
    T = Tp + Ts
    depth = ln_mix_g.shape[0]
    n_pool, page = cache_ckv.shape[1], cache_ckv.shape[2]
    past_len = page_table.shape[1] * page
    dn_alpha = (2.0 * depth) ** 0.25

    x = jnp.concatenate([x_prompt.reshape(Tp, D), x_sample.reshape(Ts, D)], axis=0)
    xb = x.astype(BF16)

    pos = jnp.concatenate([jnp.tile(jnp.arange(L, dtype=jnp.int32), B),
                           jnp.tile(past_len + jnp.arange(Ls, dtype=jnp.int32), Bs)])
    cs_k, cs_q = _rope_tables(pos)

    peer_u_b = peer_u.astype(BF16)
    peer_vt_b = jnp.swapaxes(peer_v, 1, 2).astype(BF16)
    cache_ckv2 = cache_ckv.reshape(-1, page, KV_LORA)
    cache_kpe2 = cache_kpe.reshape(-1, page, QK_ROPE)
    zeros_state = jnp.zeros((B, GLA_HEADS, GLA_DK, GLA_DV), F32)

    gla_p, gla_s, gmlp_s, ckv_out, kpe_out = [], [], [], [], []
    for l in range(depth):
        i = l // 2
        if l % 2 == 0:
            w_in, wg, w_out_o, w_out_m = _prep_even(w_in_even[i], w_gate_up[i], w_out_even[i])
            hm = mm(xb, w_in, F32)
            bg = b_gate[i].reshape(1, -1)
            gn = gla_norm_g[i].reshape(1, -1)
            o_p, s_p = gla(hm, wg, bg, gn, zeros_state, row_start=0, batch=B, seq=L)
            o_s, s_s = gla(hm, wg, bg, gn, state_gla[i], row_start=Tp, batch=Bs, seq=Ls)
            wsp_p, bsp_p = _prep_gmlp_spatial(w_spatial[i], b_spatial[i], L)
            wsp_s, bsp_s = _prep_gmlp_spatial(w_spatial[i], b_spatial[i], Ls)
            (m_p,) = gmlp(hm, gmlp_ln_g[i], gmlp_ln_b[i], wsp_p, bsp_p, row_start=0, rows=Tp, emit_v=False)
            m_s, gv_s = gmlp(hm, gmlp_ln_g[i], gmlp_ln_b[i], wsp_s, bsp_s, row_start=Tp, rows=Ts, emit_v=True)
            y = mm2(jnp.concatenate([o_p, o_s.astype(BF16)], axis=0), w_out_o, jnp.concatenate([m_p, m_s], axis=0), w_out_m)
            gla_p.append(s_p)
            gla_s.append(s_s)
            gmlp_s.append(gv_s.reshape(Bs, Ls, GMLP_GROUPS, GMLP_DG))
        else:
            w_in, w_n, w_pe, w_pe_sw, w_uk2, w_uv2, w_o = _prep_odd(w_in_odd[i], w_uq[i], w_uk[i], w_uv[i], w_o_odd[i])
            h = mm(xb, w_in, F32)
            cq, ckv, ckv_b, kpe, kpe_b = mla_post(h, cs_k, q_norm_g[i], kv_norm_g[i])
            qn = mm(cq, w_n, BF16, scale=MLA_SCALE)
            qp = q_rope(cq, w_pe, w_pe_sw, cs_q)
            kn = mm(ckv_b, w_uk2, BF16, rows=Tp)
            vv = mm(ckv_b, w_uv2, BF16, rows=Tp)
            o_p = mla_prompt_attention(qn, qp, kn, kpe_b, vv, batch=B, seq=L)
            q_lat = head_mm(qn, w_uk2, MLA_HEADS, QK_NOPE, KV_LORA, transpose_w=True, row_start=Tp, rows=Ts)
            o_lat = mla_sample_attention(page_table, q_lat.reshape(Ts, MLA_HEADS, KV_LORA),
                                         qp.reshape(T, MLA_HEADS, LANES), ckv, kpe, cache_ckv2, cache_kpe2,
                                         i * n_pool, row_start=Tp, dec_seq=Ls)
            o_s = head_mm(o_lat.reshape(Ts, MLA_HEADS * KV_LORA), w_uv2, MLA_HEADS, KV_LORA, V_HEAD, transpose_w=False)
            y = mm(jnp.concatenate([o_p, o_s], axis=0), w_o, F32)
            ckv_out.append(ckv)
            kpe_out.append(kpe[:, :QK_ROPE])
        x, xb = residual_ln(x, y, ln_mix_g[l], ln_mix_b[l], dn_alpha)
        q = mm(xb, w_peer_q[l].astype(BF16), BF16)
        keys = peer_keys[l].reshape(PEER_HEADS * 2, PEER_NKEYS, -1).astype(BF16)
        s1, c1, s2, e2, tau = peer_topk(q, keys)
        y = peer_dense(xb, peer_u_b, peer_vt_b, l, s1, c1, s2, e2, tau)
        x, xb = residual_ln(x, y, ln_ffn_g[l], ln_ffn_b[l], dn_alpha)

    def split(rows_list, width):
        p = jnp.stack([r[:Tp].reshape(B, L, width) for r in rows_list])
        s = jnp.stack([r[Tp:].reshape(Bs, Ls, width) for r in rows_list])
        return p, s

    ckv_p, ckv_s = split(ckv_out, KV_LORA)
    kpe_p, kpe_s = split(kpe_out, QK_ROPE)
    return (x[:Tp].reshape(B, L, D), x[Tp:].reshape(Bs, Ls, D),
            jnp.stack(gla_p), jnp.stack(gla_s), jnp.stack(gmlp_s),
            ckv_p, kpe_p, ckv_s, kpe_s)
```

```python
import functools

import jax
import jax.numpy as jnp
from jax import lax
from jax.experimental import pallas as pl
from jax.experimental.pallas import tpu as pltpu

F32 = jnp.float32
BF16 = jnp.bfloat16

GLA_HEADS = 4
GLA_DK = 128
GLA_DV = 256
GLA_GATE_RANK = 16
GLA_TAU = 16.0
GLA_CHUNK = 64
GMLP_GROUPS = 4
GMLP_DG = 256
GMLP_CHUNK = 128
MLA_HEADS = 16
Q_LORA = 512
KV_LORA = 512
QK_NOPE = 128
QK_ROPE = 64
V_HEAD = 128
ROPE_THETA = 10000.0
MLA_SCALE = (QK_NOPE + QK_ROPE) ** -0.5
PEER_HEADS = 8
PEER_NKEYS = 128
PEER_TOPK = 16
LN_EPS = 1e-5
RMS_EPS = 1e-6
NEG_INF = -1e30

LANES = 128
SUBLANES = 8
VMEM_LIMIT_BYTES = 56 * 1024 * 1024

GLOW_PAD = 256


def _cparams(sem):
    return pltpu.CompilerParams(dimension_semantics=sem, vmem_limit_bytes=VMEM_LIMIT_BYTES)


def _pick(n, prefs):
    for p in prefs:
        if n % p == 0:
            return p
    raise ValueError(f"no tile in {prefs} divides {n}")


def _gelu(x):
    return 0.5 * x * (1.0 + jnp.tanh(0.7978845608028654 * (x + 0.044715 * (x * x * x))))


def _dot(a, b, **kw):
    return jnp.dot(a, b, preferred_element_type=F32, **kw)


def _dot_nt(a, b, **kw):
    return lax.dot_general(a, b, (((1,), (1,)), ((), ())), preferred_element_type=F32, **kw)


def _dot_tn(a, b, **kw):
    return lax.dot_general(a, b, (((0,), (0,)), ((), ())), preferred_element_type=F32, **kw)


def _mm_kernel(x_ref, w_ref, o_ref, *, scale):
    acc = _dot(x_ref[...], w_ref[...])
    if scale != 1.0:
        acc = acc * scale
    o_ref[...] = acc.astype(o_ref.dtype)


def mm(x, w, out_dtype, *, rows=None, row_start=0, scale=1.0, tn_prefs=(1792, 1024, 768, 512, 384, 256, 128)):
    M, K = x.shape
    N = w.shape[1]
    rows = M - row_start if rows is None else rows
    tm = _pick(rows, (1024, 512, 256, 128))
    assert row_start % tm == 0
    tn = _pick(N, tn_prefs)
    r0 = row_start // tm
    return pl.pallas_call(
        functools.partial(_mm_kernel, scale=scale),
        grid=(rows // tm, N // tn),
        in_specs=[pl.BlockSpec((tm, K), lambda i, j: (i + r0, 0)),
                  pl.BlockSpec((K, tn), lambda i, j: (0, j))],
        out_specs=pl.BlockSpec((tm, tn), lambda i, j: (i, j)),
        out_shape=jax.ShapeDtypeStruct((rows, N), out_dtype),
        compiler_params=_cparams(("parallel", "parallel")),
    )(x, w)


def _mm2_kernel(x1_ref, w1_ref, x2_ref, w2_ref, o_ref):
    o_ref[...] = _dot(x1_ref[...], w1_ref[...]) + _dot(x2_ref[...], w2_ref[...])


def mm2(x1, w1, x2, w2):
    M, K1 = x1.shape
    K2 = x2.shape[1]
    N = w1.shape[1]
    tm = _pick(M, (1024, 512, 256, 128))
    tn = _pick(N, (1024, 512, 256, 128))
    return pl.pallas_call(
        _mm2_kernel,
        grid=(M // tm, N // tn),
        in_specs=[pl.BlockSpec((tm, K1), lambda i, j: (i, 0)),
                  pl.BlockSpec((K1, tn), lambda i, j: (0, j)),
                  pl.BlockSpec((tm, K2), lambda i, j: (i, 0)),
                  pl.BlockSpec((K2, tn), lambda i, j: (0, j))],
        out_specs=pl.BlockSpec((tm, tn), lambda i, j: (i, j)),
        out_shape=jax.ShapeDtypeStruct((M, N), F32),
        compiler_params=_cparams(("parallel", "parallel")),
    )(x1, w1, x2, w2)


def _headmm_kernel(x_ref, w_ref, o_ref, *, transpose_w):
    if transpose_w:
        o_ref[...] = _dot_nt(x_ref[...], w_ref[...]).astype(o_ref.dtype)
    else:
        o_ref[...] = _dot(x_ref[...], w_ref[...]).astype(o_ref.dtype)


def head_mm(x, w, heads, kin, kout, *, transpose_w, row_start=0, rows=None):
    M = x.shape[0]
    rows = M - row_start if rows is None else rows
    tm = _pick(rows, (1024, 512, 256, 128))
    assert row_start % tm == 0
    r0 = row_start // tm
    wblock = (kout, kin) if transpose_w else (kin, kout)
    return pl.pallas_call(
        functools.partial(_headmm_kernel, transpose_w=transpose_w),
        grid=(rows // tm, heads),
        in_specs=[pl.BlockSpec((tm, kin), lambda i, h: (i + r0, h)),
                  pl.BlockSpec(wblock, lambda i, h: (0, h))],
        out_specs=pl.BlockSpec((tm, kout), lambda i, h: (i, h)),
        out_shape=jax.ShapeDtypeStruct((rows, heads * kout), BF16),
        compiler_params=_cparams(("parallel", "parallel")),
    )(x, w)


def _ln_kernel(x_ref, y_ref, g_ref, b_ref, of_ref, ob_ref, *, alpha):
    z = alpha * x_ref[...] + y_ref[...]
    mu = jnp.mean(z, axis=-1, keepdims=True)
    zc = z - mu
    var = jnp.mean(zc * zc, axis=-1, keepdims=True)
    out = zc * lax.rsqrt(var + LN_EPS) * g_ref[...] + b_ref[...]
    of_ref[...] = out
    ob_ref[...] = out.astype(BF16)


def residual_ln(x, y, g, b, alpha):
    T, D = x.shape
    tm = _pick(T, (256, 128))
    row = pl.BlockSpec((tm, D), lambda i: (i, 0))
    vec = pl.BlockSpec((1, D), lambda i: (0, 0))
    return pl.pallas_call(
        functools.partial(_ln_kernel, alpha=alpha),
        grid=(T // tm,),
        in_specs=[row, row, vec, vec],
        out_specs=[row, row],
        out_shape=[jax.ShapeDtypeStruct((T, D), F32), jax.ShapeDtypeStruct((T, D), BF16)],
        compiler_params=_cparams(("parallel",)),
    )(x, y, g.reshape(1, D), b.reshape(1, D))


def _gla_kernel(q_ref, k_ref, v_ref, gl_ref, r_ref, wg_ref, bg_ref, gn_ref, s0_ref,
                o_ref, s_ref, state, *, chunk, n_chunks, mxu_dtype):
    rb = pl.program_id(2)

    @pl.when(rb == 0)
    def _():
        state[...] = s0_ref[0, 0]

    C = chunk
    row = lax.broadcasted_iota(jnp.int32, (C, C), 0)
    col = lax.broadcasted_iota(jnp.int32, (C, C), 1)
    causal = row >= col
    tri = causal.astype(F32)
    ones_cv = jnp.ones((C, GLA_DV), F32)
    hi = lax.Precision.HIGHEST
    md = mxu_dtype

    for c in range(n_chunks):
        sl = slice(c * C, (c + 1) * C)
        qi = q_ref[sl, :] * (GLA_DK ** -0.5)
        ki = k_ref[sl, :]
        vi = v_ref[sl, :]
        z = _dot(gl_ref[sl, :].astype(BF16), wg_ref[...]) + bg_ref[...]
        gi = (jnp.minimum(z, 0.0) - jnp.log(1.0 + jnp.exp(-jnp.abs(z)))) * (1.0 / GLA_TAU)
        b = _dot(tri, gi, precision=hi)
        b_last = b[C - 1:C, :]
        q_g = qi * jnp.exp(b)
        k_g = ki * jnp.exp(-b)
        att = jnp.where(causal, _dot_nt(q_g.astype(md), k_g.astype(md)), 0.0)
        s_prev = state[...]
        o = _dot(att.astype(md), vi.astype(md)) + _dot(q_g.astype(md), s_prev.astype(md))
        k_dec = ki * jnp.exp(b_last - b)
        decay = _dot_tn(gi, ones_cv, precision=hi)
        state[...] = jnp.exp(decay) * s_prev + _dot_tn(k_dec.astype(md), vi.astype(md))
        ms = jnp.mean(o * o, axis=-1, keepdims=True)
        on = o * lax.rsqrt(ms + RMS_EPS) * gn_ref[...]
        rr = r_ref[sl, :]
        o_ref[sl, :] = (on * (rr * jax.nn.sigmoid(rr))).astype(o_ref.dtype)

    @pl.when(rb == pl.num_programs(2) - 1)
    def _():
        s_ref[0, 0] = state[...]


def gla(hm, wg, bg, gn, s0, *, row_start, batch, seq):
    C = min(GLA_CHUNK, seq)
    assert seq % C == 0
    n_chunks = _pick(seq // C, (4, 2, 1))
    R = C * n_chunks
    nrb = seq // R
    assert row_start % R == 0
    r0 = row_start // R
    H = GLA_HEADS
    glow_blk = (2 * H * GLA_DK + 2 * H * GLA_DV + 2 * GMLP_GROUPS * GMLP_DG) // LANES
    rowmap = lambda off: (lambda b, h, rb: (r0 + b * nrb + rb, off + h))
    mxu_dtype = BF16 if C >= 16 else F32
    out_dtype = BF16 if R >= 16 else F32
    kernel = functools.partial(_gla_kernel, chunk=C, n_chunks=n_chunks, mxu_dtype=mxu_dtype)
    return pl.pallas_call(
        kernel,
        grid=(batch, H, nrb),
        in_specs=[
            pl.BlockSpec((R, GLA_DK), rowmap(0)),
            pl.BlockSpec((R, GLA_DK), rowmap(H)),
            pl.BlockSpec((R, GLA_DV), rowmap(H)),
            pl.BlockSpec((R, LANES), lambda b, h, rb: (r0 + b * nrb + rb, glow_blk)),
            pl.BlockSpec((R, GLA_DV), rowmap(2 * H)),
            pl.BlockSpec((LANES, GLA_DK), lambda b, h, rb: (0, h)),
            pl.BlockSpec((1, GLA_DK), lambda b, h, rb: (0, h)),
            pl.BlockSpec((1, GLA_DV), lambda b, h, rb: (0, 0)),
            pl.BlockSpec((1, 1, GLA_DK, GLA_DV), lambda b, h, rb: (b, h, 0, 0)),
        ],
        out_specs=[
            pl.BlockSpec((R, GLA_DV), lambda b, h, rb: (b * nrb + rb, h)),
            pl.BlockSpec((1, 1, GLA_DK, GLA_DV), lambda b, h, rb: (b, h, 0, 0)),
        ],
        out_shape=[jax.ShapeDtypeStruct((batch * seq, H * GLA_DV), out_dtype),
                   jax.ShapeDtypeStruct((batch, H, GLA_DK, GLA_DV), F32)],
        scratch_shapes=[pltpu.VMEM((GLA_DK, GLA_DV), F32)],
        compiler_params=_cparams(("parallel", "parallel", "arbitrary")),
    )(hm, hm, hm, hm, hm, wg, bg, gn, s0)


def _gmlp_kernel(gu_ref, gv_ref, lng_ref, lnb_ref, w_ref, b_ref, m_ref, *gvo_ref):
    u = _gelu(gu_ref[...])
    v = _gelu(gv_ref[...])
    mu = jnp.mean(v, axis=-1, keepdims=True)
    vc = v - mu
    var = jnp.mean(vc * vc, axis=-1, keepdims=True)
    vn = vc * lax.rsqrt(var + LN_EPS) * lng_ref[0] + lnb_ref[0]
    s = _dot(w_ref[0], vn.astype(BF16)) + b_ref[0]
    m_ref[...] = (u * s).astype(m_ref.dtype)
    if gvo_ref:
        gvo_ref[0][...] = vn


def gmlp(hm, lng, lnb, w_eff, b_eff, *, row_start, rows, emit_v):
    G = GMLP_GROUPS
    R = GMLP_CHUNK
    assert rows % R == 0 and row_start % R == 0
    r0 = row_start // R
    ublk = (2 * GLA_HEADS * GLA_DK + 2 * GLA_HEADS * GLA_DV) // GMLP_DG
    vblk = ublk + G
    out_block = pl.BlockSpec((R, GMLP_DG), lambda i, g: (i, g))
    out_specs = [out_block]
    out_shape = [jax.ShapeDtypeStruct((rows, G * GMLP_DG), BF16)]
    if emit_v:
        out_specs.append(out_block)
        out_shape.append(jax.ShapeDtypeStruct((rows, G * GMLP_DG), F32))
    return pl.pallas_call(
        _gmlp_kernel,
        grid=(rows // R, G),
        in_specs=[
            pl.BlockSpec((R, GMLP_DG), lambda i, g: (r0 + i, ublk + g)),
            pl.BlockSpec((R, GMLP_DG), lambda i, g: (r0 + i, vblk + g)),
            pl.BlockSpec((1, 1, GMLP_DG), lambda i, g: (g, 0, 0)),
            pl.BlockSpec((1, 1, GMLP_DG), lambda i, g: (g, 0, 0)),
            pl.BlockSpec((1, R, R), lambda i, g: (g, 0, 0)),
            pl.BlockSpec((1, R, 1), lambda i, g: (g, 0, 0)),
        ],
        out_specs=out_specs,
        out_shape=out_shape,
        compiler_params=_cparams(("parallel", "parallel")),
    )(hm, hm, lng.reshape(G, 1, GMLP_DG), lnb.reshape(G, 1, GMLP_DG), w_eff, b_eff.reshape(G, R, 1))


def _mla_post_kernel(h_ref, cs_ref, qg_ref, kg_ref, cq_ref, ckv_ref, ckvb_ref, kpe_ref, kpeb_ref):
    cq = h_ref[:, :Q_LORA]
    ckv = h_ref[:, Q_LORA:Q_LORA + KV_LORA]
    cqn = cq * lax.rsqrt(jnp.mean(cq * cq, axis=-1, keepdims=True) + RMS_EPS) * qg_ref[...]
    ckvn = ckv * lax.rsqrt(jnp.mean(ckv * ckv, axis=-1, keepdims=True) + RMS_EPS) * kg_ref[...]
    cq_ref[...] = cqn.astype(BF16)
    ckv_ref[...] = ckvn
    ckvb_ref[...] = ckvn.astype(BF16)
    t = h_ref[:, Q_LORA + KV_LORA:] * cs_ref[...]
    rot = t + pltpu.roll(t, QK_ROPE, 1)
    lane = lax.broadcasted_iota(jnp.int32, rot.shape, 1)
    rot = jnp.where(lane < QK_ROPE, rot, 0.0)
    kpe_ref[...] = rot
    kpeb_ref[...] = rot.astype(BF16)


def mla_post(h, cs, q_norm_g, kv_norm_g):
    T = h.shape[0]
    tm = _pick(T, (512, 256, 128))
    W = h.shape[1]
    return pl.pallas_call(
        _mla_post_kernel,
        grid=(T // tm,),
        in_specs=[pl.BlockSpec((tm, W), lambda i: (i, 0)),
                  pl.BlockSpec((tm, LANES), lambda i: (i, 0)),
                  pl.BlockSpec((1, Q_LORA), lambda i: (0, 0)),
                  pl.BlockSpec((1, KV_LORA), lambda i: (0, 0))],
        out_specs=[pl.BlockSpec((tm, Q_LORA), lambda i: (i, 0)),
                   pl.BlockSpec((tm, KV_LORA), lambda i: (i, 0)),
                   pl.BlockSpec((tm, KV_LORA), lambda i: (i, 0)),
                   pl.BlockSpec((tm, LANES), lambda i: (i, 0)),
                   pl.BlockSpec((tm, LANES), lambda i: (i, 0))],
        out_shape=[jax.ShapeDtypeStruct((T, Q_LORA), BF16),
                   jax.ShapeDtypeStruct((T, KV_LORA), F32),
                   jax.ShapeDtypeStruct((T, KV_LORA), BF16),
                   jax.ShapeDtypeStruct((T, LANES), F32),
                   jax.ShapeDtypeStruct((T, LANES), BF16)],
        compiler_params=_cparams(("parallel",)),
    )(h, cs, q_norm_g.reshape(1, Q_LORA), kv_norm_g.reshape(1, KV_LORA))


def _qpe_kernel(cq_ref, w_ref, wsw_ref, cs_ref, o_ref):
    a = _dot(cq_ref[...], w_ref[...])
    b = _dot(cq_ref[...], wsw_ref[...])
    cos = cs_ref[:, :LANES]
    sin = cs_ref[:, LANES:]
    for h in range(MLA_HEADS):
        sl = slice(h * LANES, (h + 1) * LANES)
        o_ref[:, sl] = ((a[:, sl] * cos + b[:, sl] * sin) * MLA_SCALE).astype(o_ref.dtype)


def q_rope(cq, w_pe, w_pe_sw, cs2):
    T, K = cq.shape
    N = w_pe.shape[1]
    tm = _pick(T, (512, 256, 128))
    return pl.pallas_call(
        _qpe_kernel,
        grid=(T // tm,),
        in_specs=[pl.BlockSpec((tm, K), lambda i: (i, 0)),
                  pl.BlockSpec((K, N), lambda i: (0, 0)),
                  pl.BlockSpec((K, N), lambda i: (0, 0)),
                  pl.BlockSpec((tm, 2 * LANES), lambda i: (i, 0))],
        out_specs=pl.BlockSpec((tm, N), lambda i: (i, 0)),
        out_shape=jax.ShapeDtypeStruct((T, N), BF16),
        compiler_params=_cparams(("parallel",)),
    )(cq, w_pe, w_pe_sw, cs2)


def _flash_kernel(qn_ref, qp_ref, kn_ref, kp_ref, v_ref, o_ref, m_scr, l_scr, acc_scr, *, tq):
    qi = pl.program_id(2)
    q = jnp.concatenate([qn_ref[...], qp_ref[...]], axis=1)
    m_scr[...] = jnp.full_like(m_scr, NEG_INF)
    l_scr[...] = jnp.zeros_like(l_scr)
    acc_scr[...] = jnp.zeros_like(acc_scr)

    def block(ki, masked):
        start = pl.multiple_of(ki * tq, tq)
        k = jnp.concatenate([kn_ref[pl.ds(start, tq), :], kp_ref[pl.ds(start, tq), :]], axis=1)
        s = _dot_nt(q, k)
        if masked:
            row = lax.broadcasted_iota(jnp.int32, s.shape, 0)
            col = lax.broadcasted_iota(jnp.int32, s.shape, 1)
            s = jnp.where(col <= row, s, NEG_INF)
        m_prev = m_scr[...]
        m_new = jnp.maximum(m_prev, jnp.max(s, axis=1, keepdims=True))
        alpha = jnp.exp(m_prev - m_new)
        p = jnp.exp(s - m_new)
        l_scr[...] = alpha * l_scr[...] + jnp.sum(p, axis=1, keepdims=True)
        acc_scr[...] = alpha * acc_scr[...] + _dot(p.astype(BF16), v_ref[pl.ds(start, tq), :])
        m_scr[...] = m_new

    def body(ki, carry):
        block(ki, False)
        return carry

    lax.fori_loop(0, qi, body, 0)
    block(qi, True)
    o_ref[...] = (acc_scr[...] / l_scr[...]).astype(o_ref.dtype)


def mla_prompt_attention(qn, qp, kn, kp, v, *, batch, seq):
    tq = _pick(seq, (512, 256, 128))
    nq = seq // tq
    H = MLA_HEADS
    return pl.pallas_call(
        functools.partial(_flash_kernel, tq=tq),
        grid=(batch, H, nq),
        in_specs=[pl.BlockSpec((tq, LANES), lambda b, h, qi: (b * nq + qi, h)),
                  pl.BlockSpec((tq, LANES), lambda b, h, qi: (b * nq + qi, h)),
                  pl.BlockSpec((seq, LANES), lambda b, h, qi: (b, h)),
                  pl.BlockSpec((seq, LANES), lambda b, h, qi: (b, 0)),
                  pl.BlockSpec((seq, LANES), lambda b, h, qi: (b, h))],
        out_specs=pl.BlockSpec((tq, LANES), lambda b, h, qi: (b * nq + qi, h)),
        out_shape=jax.ShapeDtypeStruct((batch * seq, H * V_HEAD), BF16),
        scratch_shapes=[pltpu.VMEM((tq, 1), F32), pltpu.VMEM((tq, 1), F32), pltpu.VMEM((tq, V_HEAD), F32)],
        compiler_params=_cparams(("parallel", "parallel", "arbitrary")),
    )(qn, qp, kn, kp, v)


def _paged_kernel(pt_ref, ql_ref, qp_ref, cn_ref, kn_ref, *rest, pages_per_step, dec_seq):
    P = pages_per_step
    ckv_refs = rest[:P]
    kpe_refs = rest[P:2 * P]
    o_ref = rest[2 * P]
    m_scr, l_scr, acc_scr = rest[2 * P + 1:]
    j = pl.program_id(1)
    rows = dec_seq * MLA_HEADS

    @pl.when(j == 0)
    def _():
        m_scr[...] = jnp.full_like(m_scr, NEG_INF)
        l_scr[...] = jnp.zeros_like(l_scr)
        acc_scr[...] = jnp.zeros_like(acc_scr)

    ql = ql_ref[...].reshape(rows, KV_LORA)
    qp = qp_ref[...].reshape(rows, LANES)
    qp_rope = qp[:, :QK_ROPE]

    kc = jnp.concatenate([r[0].astype(BF16) for r in ckv_refs], axis=0)
    kk = jnp.concatenate([r[0].astype(BF16) for r in kpe_refs], axis=0)
    s = _dot_nt(ql, kc) + _dot_nt(qp_rope, kk)
    m_prev = m_scr[...]
    m_new = jnp.maximum(m_prev, jnp.max(s, axis=1, keepdims=True))
    alpha = jnp.exp(m_prev - m_new)
    p = jnp.exp(s - m_new)
    l_scr[...] = alpha * l_scr[...] + jnp.sum(p, axis=1, keepdims=True)
    acc_scr[...] = alpha * acc_scr[...] + _dot(p.astype(BF16), kc)
    m_scr[...] = m_new

    @pl.when(j == pl.num_programs(1) - 1)
    def _():
        cn = cn_ref[...]
        kn = kn_ref[...]
        sn = _dot_nt(ql.astype(F32), cn) + _dot_nt(qp.astype(F32), kn)
        t_row = lax.broadcasted_iota(jnp.int32, sn.shape, 0) // MLA_HEADS
        t_col = lax.broadcasted_iota(jnp.int32, sn.shape, 1)
        sn = jnp.where(t_col <= t_row, sn, NEG_INF)
        m_prev = m_scr[...]
        m_new = jnp.maximum(m_prev, jnp.max(sn, axis=1, keepdims=True))
        alpha = jnp.exp(m_prev - m_new)
        pn = jnp.exp(sn - m_new)
        l = alpha * l_scr[...] + jnp.sum(pn, axis=1, keepdims=True)
        acc = alpha * acc_scr[...] + _dot(pn, cn)
        o_ref[...] = (acc / l).reshape(dec_seq, MLA_HEADS, KV_LORA).astype(o_ref.dtype)


def mla_sample_attention(page_table, q_lat, qp3, ckv_f32, kpe_f32, cache_ckv, cache_kpe, page_base, *, row_start, dec_seq):
    B, n_pages = page_table.shape
    page = cache_ckv.shape[1]
    P = _pick(n_pages, (8, 4, 2, 1))
    nsteps = n_pages // P
    assert row_start % dec_seq == 0
    r0 = row_start // dec_seq
    H = MLA_HEADS

    def page_map(p):
        return lambda b, j, pt: (page_base + pt[b * n_pages + j * P + p], 0, 0)

    in_specs = [
        pl.BlockSpec((dec_seq, H, KV_LORA), lambda b, j, pt: (b, 0, 0)),
        pl.BlockSpec((dec_seq, H, LANES), lambda b, j, pt: (r0 + b, 0, 0)),
        pl.BlockSpec((dec_seq, KV_LORA), lambda b, j, pt: (r0 + b, 0)),
        pl.BlockSpec((dec_seq, LANES), lambda b, j, pt: (r0 + b, 0)),
    ]
    in_specs += [pl.BlockSpec((1, page, KV_LORA), page_map(p)) for p in range(P)]
    in_specs += [pl.BlockSpec((1, page, QK_ROPE), page_map(p)) for p in range(P)]
    rows = dec_seq * H
    grid_spec = pltpu.PrefetchScalarGridSpec(
        num_scalar_prefetch=1,
        grid=(B, nsteps),
        in_specs=in_specs,
        out_specs=pl.BlockSpec((dec_seq, H, KV_LORA), lambda b, j, pt: (b, 0, 0)),
        scratch_shapes=[pltpu.VMEM((rows, 1), F32), pltpu.VMEM((rows, 1), F32), pltpu.VMEM((rows, KV_LORA), F32)],
    )
    return pl.pallas_call(
        functools.partial(_paged_kernel, pages_per_step=P, dec_seq=dec_seq),
        grid_spec=grid_spec,
        out_shape=jax.ShapeDtypeStruct((B * dec_seq, H, KV_LORA), BF16),
        compiler_params=_cparams(("parallel", "arbitrary")),
    )(page_table.reshape(-1), q_lat, qp3, ckv_f32, kpe_f32, *([cache_ckv] * P), *([cache_kpe] * P))


def _top_values(s, k):
    out = []
    for _ in range(k):
        m = jnp.max(s, axis=0, keepdims=True)
        out.append(m)
        s = jnp.where(s == m, -jnp.inf, s)
    return out


def _peer_topk_kernel(q_ref, keys_ref, s1_ref, c1_ref, s2_ref, e2_ref, tau_ref):
    K = PEER_TOPK
    for h in range(PEER_HEADS):
        qa = q_ref[:, (2 * h) * LANES:(2 * h + 1) * LANES]
        qb = q_ref[:, (2 * h + 1) * LANES:(2 * h + 2) * LANES]
        sa = _dot_nt(keys_ref[2 * h], qa)
        sb = _dot_nt(keys_ref[2 * h + 1], qb)
        ta = _top_values(sa, K)
        tb = _top_values(sb, K)
        tb_all = jnp.concatenate(tb, axis=0)
        cand = jnp.concatenate([ta[a] + tb_all for a in range(K)], axis=0)
        best = _top_values(cand, K)
        z = jnp.zeros_like(best[0])
        for r in range(K):
            z = z + jnp.exp(best[r] - best[0])
        s1_ref[h] = sa
        s2_ref[h] = sb
        c1_ref[h] = jnp.exp(sa - ta[0]) / z
        e2_ref[h] = jnp.exp(sb - tb[0])
        tau_ref[h:h + 1, :] = best[K - 1]


def peer_topk(q, keys):
    T = q.shape[0]
    TT = _pick(T, (512, 256, 128))
    blk = pl.BlockSpec((PEER_HEADS, PEER_NKEYS, TT), lambda i: (0, 0, i))
    sds = jax.ShapeDtypeStruct((PEER_HEADS, PEER_NKEYS, T), F32)
    return pl.pallas_call(
        _peer_topk_kernel,
        grid=(T // TT,),
        in_specs=[pl.BlockSpec((TT, q.shape[1]), lambda i: (i, 0)),
                  pl.BlockSpec(keys.shape, lambda i: (0, 0, 0))],
        out_specs=[blk, blk, blk, blk, pl.BlockSpec((PEER_HEADS, TT), lambda i: (0, i))],
        out_shape=[sds, sds, sds, sds, jax.ShapeDtypeStruct((PEER_HEADS, T), F32)],
        compiler_params=_cparams(("parallel",)),
    )(q, keys)


PEER_TE = 8 * PEER_NKEYS


def _peer_dense_kernel(x_ref, u_ref, vt_ref, s1_ref, c1_ref, s2_ref, e2_ref, tau_ref, o_ref, acc_scr, a_scr):
    j = pl.program_id(1)

    @pl.when(j == 0)
    def _():
        acc_scr[...] = jnp.zeros_like(acc_scr)

    x = x_ref[...]
    gps = 2
    for g in range(PEER_TE // PEER_NKEYS):
        if g % gps == 0:
            slab = slice(g * PEER_NKEYS, (g + gps) * PEER_NKEYS)
            ht = _dot_nt(u_ref[0, slab, :], x)
        gate = None
        for h in range(PEER_HEADS):
            s1 = s1_ref[h, g:g + 1, :]
            c1 = c1_ref[h, g:g + 1, :]
            total = s2_ref[h] + s1
            term = jnp.where(total >= tau_ref[h:h + 1, :], e2_ref[h] * c1, 0.0)
            gate = term if gate is None else gate + term
        hs = ht[(g % gps) * PEER_NKEYS:(g % gps + 1) * PEER_NKEYS, :]
        a_scr[g * PEER_NKEYS:(g + 1) * PEER_NKEYS, :] = (_gelu(hs) * gate).astype(BF16)
    acc_scr[...] += _dot(vt_ref[0], a_scr[...])

    @pl.when(j == pl.num_programs(1) - 1)
    def _():
        o_ref[...] = acc_scr[...].T


def peer_dense(x, u, vt, layer, s1, c1, s2, e2, tau):
    T, D = x.shape
    E = u.shape[1]
    TT = _pick(T, (512, 256, 128))
    TE = PEER_TE
    assert E % TE == 0 and E == PEER_NKEYS * PEER_NKEYS
    full = pl.BlockSpec((PEER_HEADS, PEER_NKEYS, TT), lambda i, j: (0, 0, i))
    rows8 = pl.BlockSpec((PEER_HEADS, SUBLANES, TT), lambda i, j: (0, j, i))
    return pl.pallas_call(
        _peer_dense_kernel,
        grid=(T // TT, E // TE),
        in_specs=[pl.BlockSpec((TT, D), lambda i, j: (i, 0)),
                  pl.BlockSpec((1, TE, D), lambda i, j: (layer, j, 0)),
                  pl.BlockSpec((1, D, TE), lambda i, j: (layer, 0, j)),
                  rows8, rows8, full, full,
                  pl.BlockSpec((PEER_HEADS, TT), lambda i, j: (0, i))],
        out_specs=pl.BlockSpec((TT, D), lambda i, j: (i, 0)),
        out_shape=jax.ShapeDtypeStruct((T, D), F32),
        scratch_shapes=[pltpu.VMEM((D, TT), F32), pltpu.VMEM((TE, TT), BF16)],
        compiler_params=_cparams(("parallel", "arbitrary")),
    )(x, u, vt, s1, c1, s2, e2, tau)


def _cast_transpose_kernel(v_ref, o_ref):
    o_ref[0] = v_ref[0].T.astype(o_ref.dtype)


def cast_transpose(v):
    L, E, D = v.shape
    te = _pick(E, (512, 256, 128))
    return pl.pallas_call(
        _cast_transpose_kernel,
        grid=(L, E // te),
        in_specs=[pl.BlockSpec((1, te, D), lambda l, j: (l, j, 0))],
        out_specs=pl.BlockSpec((1, D, te), lambda l, j: (l, 0, j)),
        out_shape=jax.ShapeDtypeStruct((L, D, E), BF16),
        compiler_params=_cparams(("parallel", "parallel")),
    )(v)


def _rope_swap_cols(w):
    half = QK_ROPE // 2
    return jnp.concatenate([-w[..., half:], w[..., :half]], axis=-1)


def _prep_even(w_in, w_gate_up, w_out):
    qk = 2 * GLA_HEADS * GLA_DK
    vw = GLA_HEADS * GLA_DV
    g0 = qk + vw
    g1 = g0 + GLA_GATE_RANK
    K = w_in.shape[0]
    w_perm = jnp.concatenate(
        [w_in[:, :g0], w_in[:, g1:], w_in[:, g0:g1], jnp.zeros((K, GLOW_PAD - GLA_GATE_RANK), w_in.dtype)], axis=1)
    wg = jnp.concatenate([w_gate_up, jnp.zeros((LANES - GLA_GATE_RANK, w_gate_up.shape[1]), w_gate_up.dtype)], axis=0)
    return w_perm.astype(BF16), wg.astype(BF16), w_out[:vw].astype(BF16), w_out[vw:].astype(BF16)


def _prep_gmlp_spatial(w_sp, b_sp, seq):
    C = min(GMLP_CHUNK, seq)
    assert seq % C == 0 and GMLP_CHUNK % C == 0
    w = jnp.where(jnp.tril(jnp.ones((C, C), bool)), w_sp[:, :C, :C], 0.0)
    b = b_sp[:, :C]
    rep = GMLP_CHUNK // C
    if rep > 1:
        eye = jnp.eye(rep, dtype=w.dtype)
        w = jnp.einsum('ab,gij->gaibj', eye, w).reshape(w.shape[0], GMLP_CHUNK, GMLP_CHUNK)
        b = jnp.tile(b, (1, rep))
    return w.astype(BF16), b


def _prep_odd(w_in, w_uq, w_uk, w_uv, w_o):
    kpe_w = w_in[:, Q_LORA + KV_LORA:]
    w_in_p = jnp.concatenate([w_in, _rope_swap_cols(kpe_w)], axis=1).astype(BF16)
    w_n = w_uq[:, :, :QK_NOPE].reshape(Q_LORA, MLA_HEADS * QK_NOPE).astype(BF16)
    pe = w_uq[:, :, QK_NOPE:]
    zpad = jnp.zeros((Q_LORA, MLA_HEADS, LANES - QK_ROPE), w_uq.dtype)
    w_pe = jnp.concatenate([pe, zpad], axis=-1).reshape(Q_LORA, MLA_HEADS * LANES).astype(BF16)
    w_pe_sw = jnp.concatenate([_rope_swap_cols(pe), zpad], axis=-1).reshape(Q_LORA, MLA_HEADS * LANES).astype(BF16)
    w_uk2 = w_uk.reshape(KV_LORA, MLA_HEADS * QK_NOPE).astype(BF16)
    w_uv2 = w_uv.reshape(KV_LORA, MLA_HEADS * V_HEAD).astype(BF16)
    return w_in_p, w_n, w_pe, w_pe_sw, w_uk2, w_uv2, w_o.astype(BF16)


def _rope_tables(pos):
    inv = ROPE_THETA ** (-jnp.arange(0, QK_ROPE, 2, dtype=F32) / QK_ROPE)
    ang = pos.astype(F32)[:, None] * inv[None, :]
    cos, sin = jnp.cos(ang), jnp.sin(ang)
    z = jnp.zeros((pos.shape[0], LANES - QK_ROPE), F32)
    cs_k = jnp.concatenate([cos, cos, sin, sin], axis=1)
    cs_q = jnp.concatenate([cos, cos, z, sin, sin, z], axis=1)
    return cs_k, cs_q


def kernel(x_prompt, x_sample, state_gla, cache_ckv, cache_kpe, page_table, w_in_even, w_gate_up, b_gate, gla_norm_g, gmlp_ln_g, gmlp_ln_b, w_spatial, b_spatial, w_out_even, w_in_odd, q_norm_g, kv_norm_g, w_uq, w_uk, w_uv, w_o_odd, ln_mix_g, ln_mix_b, ln_ffn_g, ln_ffn_b, w_peer_q, peer_keys, peer_u, peer_v):
    B, L, D = x_prompt.shape
    Bs, Ls, _ = x_sample.shape
    Tp, Ts = B * L, Bs * Ls
    T = Tp + Ts
    depth = ln_mix_g.shape[0]
    n_pool, page = cache_ckv.shape[1], cache_ckv.shape[2]
    past_len = page_table.shape[1] * page
    dn_alpha = (2.0 * depth) ** 0.25

    x = jnp.concatenate([x_prompt.reshape(Tp, D), x_sample.reshape(Ts, D)], axis=0)
    xb = x.astype(BF16)

    pos = jnp.concatenate([jnp.tile(jnp.arange(L, dtype=jnp.int32), B),
                           jnp.tile(past_len + jnp.arange(Ls, dtype=jnp.int32), Bs)])
    cs_k, cs_q = _rope_tables(pos)

    peer_u_b = peer_u.astype(BF16)
    peer_vt_b = cast_transpose(peer_v)
    cache_ckv2 = cache_ckv.reshape(-1, page, KV_LORA)
    cache_kpe2 = cache_kpe.reshape(-1, page, QK_ROPE)
    zeros_state = jnp.zeros((B, GLA_HEADS, GLA_DK, GLA_DV), F32)

    gla_p, gla_s, gmlp_s, ckv_out, kpe_out = [], [], [], [], []
    for l in range(depth):
        i = l // 2
        if l % 2 == 0:
            w_in, wg, w_out_o, w_out_m = _prep_even(w_in_even[i], w_gate_up[i], w_out_even[i])
            hm = mm(xb, w_in, F32)
            bg = b_gate[i].reshape(1, -1)
            gn = gla_norm_g[i].reshape(1, -1)
            o_p, s_p = gla(hm, wg, bg, gn, zeros_state, row_start=0, batch=B, seq=L)
            o_s, s_s = gla(hm, wg, bg, gn, state_gla[i], row_start=Tp, batch=Bs, seq=Ls)
            wsp_p, bsp_p = _prep_gmlp_spatial(w_spatial[i], b_spatial[i], L)
            wsp_s, bsp_s = _prep_gmlp_spatial(w_spatial[i], b_spatial[i], Ls)
            (m_p,) = gmlp(hm, gmlp_ln_g[i], gmlp_ln_b[i], wsp_p, bsp_p, row_start=0, rows=Tp, emit_v=False)
            m_s, gv_s = gmlp(hm, gmlp_ln_g[i], gmlp_ln_b[i], wsp_s, bsp_s, row_start=Tp, rows=Ts, emit_v=True)
            y = mm2(jnp.concatenate([o_p, o_s.astype(BF16)], axis=0), w_out_o, jnp.concatenate([m_p, m_s], axis=0), w_out_m)
            gla_p.append(s_p)
            gla_s.append(s_s)
            gmlp_s.append(gv_s.reshape(Bs, Ls, GMLP_GROUPS, GMLP_DG))
        else:
            w_in, w_n, w_pe, w_pe_sw, w_uk2, w_uv2, w_o = _prep_odd(w_in_odd[i], w_uq[i], w_uk[i], w_uv[i], w_o_odd[i])
            h = mm(xb, w_in, F32)
            cq, ckv, ckv_b, kpe, kpe_b = mla_post(h, cs_k, q_norm_g[i], kv_norm_g[i])
            qn = mm(cq, w_n, BF16, scale=MLA_SCALE)
            qp = q_rope(cq, w_pe, w_pe_sw, cs_q)
            kn = mm(ckv_b, w_uk2, BF16, rows=Tp)
            vv = mm(ckv_b, w_uv2, BF16, rows=Tp)
            o_p = mla_prompt_attention(qn, qp, kn, kpe_b, vv, batch=B, seq=L)
            q_lat = head_mm(qn, w_uk2, MLA_HEADS, QK_NOPE, KV_LORA, transpose_w=True, row_start=Tp, rows=Ts)
            o_lat = mla_sample_attention(page_table, q_lat.reshape(Ts, MLA_HEADS, KV_LORA),
                                         qp.reshape(T, MLA_HEADS, LANES), ckv, kpe, cache_ckv2, cache_kpe2,
                                         i * n_pool, row_start=Tp, dec_seq=Ls)
            o_s = head_mm(o_lat.reshape(Ts, MLA_HEADS * KV_LORA), w_uv2, MLA_HEADS, KV_LORA, V_HEAD, transpose_w=False)
            y = mm(jnp.concatenate([o_p, o_s], axis=0), w_o, F32)
            ckv_out.append(ckv)
            kpe_out.append(kpe[:, :QK_ROPE])
        x, xb = residual_ln(x, y, ln_mix_g[l], ln_mix_b[l], dn_alpha)
        q = mm(xb, w_peer_q[l].astype(BF16), BF16)
        keys = peer_keys[l].reshape(PEER_HEADS * 2, PEER_NKEYS, -1).astype(BF16)
        s1, c1, s2, e2, tau = peer_topk(q, keys)
        y = peer_dense(xb, peer_u_b, peer_vt_b, l, s1, c1, s2, e2, tau)
        x, xb = residual_ln(x, y, ln_ffn_g[l], ln_ffn_b[l], dn_alpha)

    def split(rows_list, width):
        p = jnp.stack([r[:Tp].reshape(B, L, width) for r in rows_list])
        s = jnp.stack([r[Tp:].reshape(Bs, Ls, width) for r in rows_list])
        return p, s

    ckv_p, ckv_s = split(ckv_out, KV_LORA)
    kpe_p, kpe_s = split(kpe_out, QK_ROPE)
    return (x[:Tp].reshape(B, L, D), x[Tp:].reshape(Bs, Ls, D),
            jnp.stack(gla_p), jnp.stack(gla_s), jnp.stack(gmlp_s),
            ckv_p, kpe_p, ckv_s, kpe_s)
```

```python
import functools

import jax
import jax.numpy as jnp
from jax import lax
from jax.experimental import pallas as pl
from jax.experimental.pallas import tpu as pltpu

F32 = jnp.float32
BF16 = jnp.bfloat16

GLA_HEADS = 4
GLA_DK = 128
GLA_DV = 256
GLA_GATE_RANK = 16
GLA_TAU = 16.0
GLA_CHUNK = 64
GMLP_GROUPS = 4
GMLP_DG = 256
GMLP_CHUNK = 128
MLA_HEADS = 16
Q_LORA = 512
KV_LORA = 512
QK_NOPE = 128
QK_ROPE = 64
V_HEAD = 128
ROPE_THETA = 10000.0
MLA_SCALE = (QK_NOPE + QK_ROPE) ** -0.5
PEER_HEADS = 8
PEER_NKEYS = 128
PEER_TOPK = 16
LN_EPS = 1e-5
RMS_EPS = 1e-6
NEG_INF = -1e30

LANES = 128
SUBLANES = 8
VMEM_LIMIT_BYTES = 56 * 1024 * 1024

GLOW_PAD = 256


def _cparams(sem):
    return pltpu.CompilerParams(dimension_semantics=sem, vmem_limit_bytes=VMEM_LIMIT_BYTES)


def _pick(n, prefs):
    for p in prefs:
        if n % p == 0:
            return p
    raise ValueError(f"no tile in {prefs} divides {n}")


def _gelu(x):
    return 0.5 * x * (1.0 + jnp.tanh(0.7978845608028654 * (x + 0.044715 * (x * x * x))))


def _dot(a, b, **kw):
    return jnp.dot(a, b, preferred_element_type=F32, **kw)


def _dot_nt(a, b, **kw):
    return lax.dot_general(a, b, (((1,), (1,)), ((), ())), preferred_element_type=F32, **kw)


def _dot_tn(a, b, **kw):
    return lax.dot_general(a, b, (((0,), (0,)), ((), ())), preferred_element_type=F32, **kw)


def _mm_kernel(x_ref, w_ref, o_ref, *, scale):
    acc = _dot(x_ref[...], w_ref[...])
    if scale != 1.0:
        acc = acc * scale
    o_ref[...] = acc.astype(o_ref.dtype)


def mm(x, w, out_dtype, *, rows=None, row_start=0, scale=1.0, tn_prefs=(1792, 1024, 768, 512, 384, 256, 128)):
    M, K = x.shape
    N = w.shape[1]
    rows = M - row_start if rows is None else rows
    tm = _pick(rows, (1024, 512, 256, 128))
    assert row_start % tm == 0
    tn = _pick(N, tn_prefs)
    r0 = row_start // tm
    return pl.pallas_call(
        functools.partial(_mm_kernel, scale=scale),
        grid=(rows // tm, N // tn),
        in_specs=[pl.BlockSpec((tm, K), lambda i, j: (i + r0, 0)),
                  pl.BlockSpec((K, tn), lambda i, j: (0, j))],
        out_specs=pl.BlockSpec((tm, tn), lambda i, j: (i, j)),
        out_shape=jax.ShapeDtypeStruct((rows, N), out_dtype),
        compiler_params=_cparams(("parallel", "parallel")),
    )(x, w)


def _mm2_kernel(x1_ref, w1_ref, x2_ref, w2_ref, o_ref):
    o_ref[...] = _dot(x1_ref[...], w1_ref[...]) + _dot(x2_ref[...], w2_ref[...])


def mm2(x1, w1, x2, w2):
    M, K1 = x1.shape
    K2 = x2.shape[1]
    N = w1.shape[1]
    tm = _pick(M, (1024, 512, 256, 128))
    tn = _pick(N, (1024, 512, 256, 128))
    return pl.pallas_call(
        _mm2_kernel,
        grid=(M // tm, N // tn),
        in_specs=[pl.BlockSpec((tm, K1), lambda i, j: (i, 0)),
                  pl.BlockSpec((K1, tn), lambda i, j: (0, j)),
                  pl.BlockSpec((tm, K2), lambda i, j: (i, 0)),
                  pl.BlockSpec((K2, tn), lambda i, j: (0, j))],
        out_specs=pl.BlockSpec((tm, tn), lambda i, j: (i, j)),
        out_shape=jax.ShapeDtypeStruct((M, N), F32),
        compiler_params=_cparams(("parallel", "parallel")),
    )(x1, w1, x2, w2)


def _headmm_kernel(x_ref, w_ref, o_ref, *, transpose_w):
    if transpose_w:
        o_ref[...] = _dot_nt(x_ref[...], w_ref[...]).astype(o_ref.dtype)
    else:
        o_ref[...] = _dot(x_ref[...], w_ref[...]).astype(o_ref.dtype)


def head_mm(x, w, heads, kin, kout, *, transpose_w, row_start=0, rows=None):
    M = x.shape[0]
    rows = M - row_start if rows is None else rows
    tm = _pick(rows, (1024, 512, 256, 128))
    assert row_start % tm == 0
    r0 = row_start // tm
    wblock = (kout, kin) if transpose_w else (kin, kout)
    return pl.pallas_call(
        functools.partial(_headmm_kernel, transpose_w=transpose_w),
        grid=(rows // tm, heads),
        in_specs=[pl.BlockSpec((tm, kin), lambda i, h: (i + r0, h)),
                  pl.BlockSpec(wblock, lambda i, h: (0, h))],
        out_specs=pl.BlockSpec((tm, kout), lambda i, h: (i, h)),
        out_shape=jax.ShapeDtypeStruct((rows, heads * kout), BF16),
        compiler_params=_cparams(("parallel", "parallel")),
    )(x, w)


def _ln_kernel(x_ref, y_ref, g_ref, b_ref, of_ref, ob_ref, *, alpha):
    z = alpha * x_ref[...] + y_ref[...]
    mu = jnp.mean(z, axis=-1, keepdims=True)
    zc = z - mu
    var = jnp.mean(zc * zc, axis=-1, keepdims=True)
    out = zc * lax.rsqrt(var + LN_EPS) * g_ref[...] + b_ref[...]
    of_ref[...] = out
    ob_ref[...] = out.astype(BF16)


def residual_ln(x, y, g, b, alpha):
    T, D = x.shape
    tm = _pick(T, (256, 128))
    row = pl.BlockSpec((tm, D), lambda i: (i, 0))
    vec = pl.BlockSpec((1, D), lambda i: (0, 0))
    return pl.pallas_call(
        functools.partial(_ln_kernel, alpha=alpha),
        grid=(T // tm,),
        in_specs=[row, row, vec, vec],
        out_specs=[row, row],
        out_shape=[jax.ShapeDtypeStruct((T, D), F32), jax.ShapeDtypeStruct((T, D), BF16)],
        compiler_params=_cparams(("parallel",)),
    )(x, y, g.reshape(1, D), b.reshape(1, D))


def _gla_kernel(q_ref, k_ref, v_ref, gl_ref, r_ref, wg_ref, bg_ref, gn_ref, s0_ref,
                o_ref, s_ref, state, *, chunk, n_chunks, mxu_dtype):
    rb = pl.program_id(1)

    @pl.when(rb == 0)
    def _():
        state[...] = s0_ref[0]

    C = chunk
    row = lax.broadcasted_iota(jnp.int32, (C, C), 0)
    col = lax.broadcasted_iota(jnp.int32, (C, C), 1)
    causal = row >= col
    tri = causal.astype(F32)
    ones_cv = jnp.ones((C, GLA_DV), F32)
    hi = lax.Precision.HIGHEST
    md = mxu_dtype

    for c in range(n_chunks):
        sl = slice(c * C, (c + 1) * C)
        glow = gl_ref[sl, :].astype(BF16)
        outs, states = [], []
        for h in range(GLA_HEADS):
            kcols = slice(h * GLA_DK, (h + 1) * GLA_DK)
            vcols = slice(h * GLA_DV, (h + 1) * GLA_DV)
            qi = q_ref[sl, kcols] * (GLA_DK ** -0.5)
            ki = k_ref[sl, kcols]
            vi = v_ref[sl, vcols]
            z = _dot(glow, wg_ref[:, kcols]) + bg_ref[:, kcols]
            gi = (jnp.minimum(z, 0.0) - jnp.log(1.0 + jnp.exp(-jnp.abs(z)))) * (1.0 / GLA_TAU)
            b = _dot(tri, gi, precision=hi)
            b_last = b[C - 1:C, :]
            q_g = qi * jnp.exp(b)
            k_g = ki * jnp.exp(-b)
            att = jnp.where(causal, _dot_nt(q_g.astype(md), k_g.astype(md)), 0.0)
            s_prev = state[h]
            o = _dot(att.astype(md), vi.astype(md)) + _dot(q_g.astype(md), s_prev.astype(md))
            k_dec = ki * jnp.exp(b_last - b)
            decay = _dot_tn(gi, ones_cv, precision=hi)
            states.append(jnp.exp(decay) * s_prev + _dot_tn(k_dec.astype(md), vi.astype(md)))
            ms = jnp.mean(o * o, axis=-1, keepdims=True)
            on = o * lax.rsqrt(ms + RMS_EPS) * gn_ref[...]
            rr = r_ref[sl, vcols]
            outs.append((on * (rr * jax.nn.sigmoid(rr))).astype(o_ref.dtype))
        for h in range(GLA_HEADS):
            state[h] = states[h]
            o_ref[sl, h * GLA_DV:(h + 1) * GLA_DV] = outs[h]

    @pl.when(rb == pl.num_programs(1) - 1)
    def _():
        s_ref[0] = state[...]


def gla(hm, wg, bg, gn, s0, *, row_start, batch, seq):
    C = min(GLA_CHUNK, seq)
    assert seq % C == 0
    n_chunks = _pick(seq // C, (4, 2, 1))
    R = C * n_chunks
    nrb = seq // R
    assert row_start % R == 0
    r0 = row_start // R
    H = GLA_HEADS
    QW, VW = H * GLA_DK, H * GLA_DV
    glow_blk = (2 * QW + 2 * VW + 2 * GMLP_GROUPS * GMLP_DG) // LANES
    rowmap = lambda blk: (lambda b, rb: (r0 + b * nrb + rb, blk))
    mxu_dtype = BF16 if C >= 16 else F32
    out_dtype = BF16 if R >= 16 else F32
    kernel = functools.partial(_gla_kernel, chunk=C, n_chunks=n_chunks, mxu_dtype=mxu_dtype)
    return pl.pallas_call(
        kernel,
        grid=(batch, nrb),
        in_specs=[
            pl.BlockSpec((R, QW), rowmap(0)),
            pl.BlockSpec((R, QW), rowmap(1)),
            pl.BlockSpec((R, VW), rowmap(2 * QW // VW)),
            pl.BlockSpec((R, LANES), rowmap(glow_blk)),
            pl.BlockSpec((R, VW), rowmap(2 * QW // VW + 1)),
            pl.BlockSpec((LANES, QW), lambda b, rb: (0, 0)),
            pl.BlockSpec((1, QW), lambda b, rb: (0, 0)),
            pl.BlockSpec((1, GLA_DV), lambda b, rb: (0, 0)),
            pl.BlockSpec((1, H, GLA_DK, GLA_DV), lambda b, rb: (b, 0, 0, 0)),
        ],
        out_specs=[
            pl.BlockSpec((R, VW), lambda b, rb: (b * nrb + rb, 0)),
            pl.BlockSpec((1, H, GLA_DK, GLA_DV), lambda b, rb: (b, 0, 0, 0)),
        ],
        out_shape=[jax.ShapeDtypeStruct((batch * seq, VW), out_dtype),
                   jax.ShapeDtypeStruct((batch, H, GLA_DK, GLA_DV), F32)],
        scratch_shapes=[pltpu.VMEM((H, GLA_DK, GLA_DV), F32)],
        compiler_params=_cparams(("parallel", "arbitrary")),
    )(hm, hm, hm, hm, hm, wg, bg, gn, s0)


def _gmlp_kernel(gu_ref, gv_ref, lng_ref, lnb_ref, w_ref, b_ref, m_ref, *gvo_ref):
    u = _gelu(gu_ref[...])
    v = _gelu(gv_ref[...])
    mu = jnp.mean(v, axis=-1, keepdims=True)
    vc = v - mu
    var = jnp.mean(vc * vc, axis=-1, keepdims=True)
    vn = vc * lax.rsqrt(var + LN_EPS) * lng_ref[0] + lnb_ref[0]
    s = _dot(w_ref[0], vn.astype(BF16)) + b_ref[0]
    m_ref[...] = (u * s).astype(m_ref.dtype)
    if gvo_ref:
        gvo_ref[0][...] = vn


def gmlp(hm, lng, lnb, w_eff, b_eff, *, row_start, rows, emit_v):
    G = GMLP_GROUPS
    R = GMLP_CHUNK
    assert rows % R == 0 and row_start % R == 0
    r0 = row_start // R
    ublk = (2 * GLA_HEADS * GLA_DK + 2 * GLA_HEADS * GLA_DV) // GMLP_DG
    vblk = ublk + G
    out_block = pl.BlockSpec((R, GMLP_DG), lambda i, g: (i, g))
    out_specs = [out_block]
    out_shape = [jax.ShapeDtypeStruct((rows, G * GMLP_DG), BF16)]
    if emit_v:
        out_specs.append(out_block)
        out_shape.append(jax.ShapeDtypeStruct((rows, G * GMLP_DG), F32))
    return pl.pallas_call(
        _gmlp_kernel,
        grid=(rows // R, G),
        in_specs=[
            pl.BlockSpec((R, GMLP_DG), lambda i, g: (r0 + i, ublk + g)),
            pl.BlockSpec((R, GMLP_DG), lambda i, g: (r0 + i, vblk + g)),
            pl.BlockSpec((1, 1, GMLP_DG), lambda i, g: (g, 0, 0)),
            pl.BlockSpec((1, 1, GMLP_DG), lambda i, g: (g, 0, 0)),
            pl.BlockSpec((1, R, R), lambda i, g: (g, 0, 0)),
            pl.BlockSpec((1, R, 1), lambda i, g: (g, 0, 0)),
        ],
        out_specs=out_specs,
        out_shape=out_shape,
        compiler_params=_cparams(("parallel", "parallel")),
    )(hm, hm, lng.reshape(G, 1, GMLP_DG), lnb.reshape(G, 1, GMLP_DG), w_eff, b_eff.reshape(G, R, 1))


def _mla_post_kernel(h_ref, cs_ref, qg_ref, kg_ref, cq_ref, ckv_ref, ckvb_ref, kpe_ref, kpeb_ref):
    cq = h_ref[:, :Q_LORA]
    ckv = h_ref[:, Q_LORA:Q_LORA + KV_LORA]
    cqn = cq * lax.rsqrt(jnp.mean(cq * cq, axis=-1, keepdims=True) + RMS_EPS) * qg_ref[...]
    ckvn = ckv * lax.rsqrt(jnp.mean(ckv * ckv, axis=-1, keepdims=True) + RMS_EPS) * kg_ref[...]
    cq_ref[...] = cqn.astype(BF16)
    ckv_ref[...] = ckvn
    ckvb_ref[...] = ckvn.astype(BF16)
    t = h_ref[:, Q_LORA + KV_LORA:] * cs_ref[...]
    rot = t + pltpu.roll(t, QK_ROPE, 1)
    lane = lax.broadcasted_iota(jnp.int32, rot.shape, 1)
    rot = jnp.where(lane < QK_ROPE, rot, 0.0)
    kpe_ref[...] = rot
    kpeb_ref[...] = rot.astype(BF16)


def mla_post(h, cs, q_norm_g, kv_norm_g):
    T = h.shape[0]
    tm = _pick(T, (512, 256, 128))
    W = h.shape[1]
    return pl.pallas_call(
        _mla_post_kernel,
        grid=(T // tm,),
        in_specs=[pl.BlockSpec((tm, W), lambda i: (i, 0)),
                  pl.BlockSpec((tm, LANES), lambda i: (i, 0)),
                  pl.BlockSpec((1, Q_LORA), lambda i: (0, 0)),
                  pl.BlockSpec((1, KV_LORA), lambda i: (0, 0))],
        out_specs=[pl.BlockSpec((tm, Q_LORA), lambda i: (i, 0)),
                   pl.BlockSpec((tm, KV_LORA), lambda i: (i, 0)),
                   pl.BlockSpec((tm, KV_LORA), lambda i: (i, 0)),
                   pl.BlockSpec((tm, LANES), lambda i: (i, 0)),
                   pl.BlockSpec((tm, LANES), lambda i: (i, 0))],
        out_shape=[jax.ShapeDtypeStruct((T, Q_LORA), BF16),
                   jax.ShapeDtypeStruct((T, KV_LORA), F32),
                   jax.ShapeDtypeStruct((T, KV_LORA), BF16),
                   jax.ShapeDtypeStruct((T, LANES), F32),
                   jax.ShapeDtypeStruct((T, LANES), BF16)],
        compiler_params=_cparams(("parallel",)),
    )(h, cs, q_norm_g.reshape(1, Q_LORA), kv_norm_g.reshape(1, KV_LORA))


def _qpe_kernel(cq_ref, w_ref, wsw_ref, cs_ref, o_ref):
    a = _dot(cq_ref[...], w_ref[...])
    b = _dot(cq_ref[...], wsw_ref[...])
    cos = cs_ref[:, :LANES]
    sin = cs_ref[:, LANES:]
    for h in range(MLA_HEADS):
        sl = slice(h * LANES, (h + 1) * LANES)
        o_ref[:, sl] = ((a[:, sl] * cos + b[:, sl] * sin) * MLA_SCALE).astype(o_ref.dtype)


def q_rope(cq, w_pe, w_pe_sw, cs2):
    T, K = cq.shape
    N = w_pe.shape[1]
    tm = _pick(T, (512, 256, 128))
    return pl.pallas_call(
        _qpe_kernel,
        grid=(T // tm,),
        in_specs=[pl.BlockSpec((tm, K), lambda i: (i, 0)),
                  pl.BlockSpec((K, N), lambda i: (0, 0)),
                  pl.BlockSpec((K, N), lambda i: (0, 0)),
                  pl.BlockSpec((tm, 2 * LANES), lambda i: (i, 0))],
        out_specs=pl.BlockSpec((tm, N), lambda i: (i, 0)),
        out_shape=jax.ShapeDtypeStruct((T, N), BF16),
        compiler_params=_cparams(("parallel",)),
    )(cq, w_pe, w_pe_sw, cs2)


def _flash_kernel(qn_ref, qp_ref, kn_ref, kp_ref, v_ref, o_ref, m_scr, l_scr, acc_scr, *, tq):
    qi = pl.program_id(2)
    q = jnp.concatenate([qn_ref[...], qp_ref[...]], axis=1)
    m_scr[...] = jnp.full_like(m_scr, NEG_INF)
    l_scr[...] = jnp.zeros_like(l_scr)
    acc_scr[...] = jnp.zeros_like(acc_scr)

    def block(ki, masked):
        start = pl.multiple_of(ki * tq, tq)
        k = jnp.concatenate([kn_ref[pl.ds(start, tq), :], kp_ref[pl.ds(start, tq), :]], axis=1)
        s = _dot_nt(q, k)
        if masked:
            row = lax.broadcasted_iota(jnp.int32, s.shape, 0)
            col = lax.broadcasted_iota(jnp.int32, s.shape, 1)
            s = jnp.where(col <= row, s, NEG_INF)
        m_prev = m_scr[...]
        m_new = jnp.maximum(m_prev, jnp.max(s, axis=1, keepdims=True))
        alpha = jnp.exp(m_prev - m_new)
        p = jnp.exp(s - m_new)
        l_scr[...] = alpha * l_scr[...] + jnp.sum(p, axis=1, keepdims=True)
        acc_scr[...] = alpha * acc_scr[...] + _dot(p.astype(BF16), v_ref[pl.ds(start, tq), :])
        m_scr[...] = m_new

    def body(ki, carry):
        block(ki, False)
        return carry

    lax.fori_loop(0, qi, body, 0)
    block(qi, True)
    o_ref[...] = (acc_scr[...] / l_scr[...]).astype(o_ref.dtype)


def mla_prompt_attention(qn, qp, kn, kp, v, *, batch, seq):
    tq = _pick(seq, (512, 256, 128))
    nq = seq // tq
    H = MLA_HEADS
    return pl.pallas_call(
        functools.partial(_flash_kernel, tq=tq),
        grid=(batch, H, nq),
        in_specs=[pl.BlockSpec((tq, LANES), lambda b, h, qi: (b * nq + qi, h)),
                  pl.BlockSpec((tq, LANES), lambda b, h, qi: (b * nq + qi, h)),
                  pl.BlockSpec((seq, LANES), lambda b, h, qi: (b, h)),
                  pl.BlockSpec((seq, LANES), lambda b, h, qi: (b, 0)),
                  pl.BlockSpec((seq, LANES), lambda b, h, qi: (b, h))],
        out_specs=pl.BlockSpec((tq, LANES), lambda b, h, qi: (b * nq + qi, h)),
        out_shape=jax.ShapeDtypeStruct((batch * seq, H * V_HEAD), BF16),
        scratch_shapes=[pltpu.VMEM((tq, 1), F32), pltpu.VMEM((tq, 1), F32), pltpu.VMEM((tq, V_HEAD), F32)],
        compiler_params=_cparams(("parallel", "parallel", "arbitrary")),
    )(qn, qp, kn, kp, v)


def _paged_kernel(pt_ref, ql_ref, qp_ref, cn_ref, kn_ref, *rest, pages_per_step, dec_seq):
    P = pages_per_step
    ckv_refs = rest[:P]
    kpe_refs = rest[P:2 * P]
    o_ref = rest[2 * P]
    m_scr, l_scr, acc_scr = rest[2 * P + 1:]
    j = pl.program_id(1)
    rows = dec_seq * MLA_HEADS

    @pl.when(j == 0)
    def _():
        m_scr[...] = jnp.full_like(m_scr, NEG_INF)
        l_scr[...] = jnp.zeros_like(l_scr)
        acc_scr[...] = jnp.zeros_like(acc_scr)

    ql = ql_ref[...].reshape(rows, KV_LORA)
    qp = qp_ref[...].reshape(rows, LANES)
    qp_rope = qp[:, :QK_ROPE]

    kc = jnp.concatenate([r[0].astype(BF16) for r in ckv_refs], axis=0)
    kk_t = jnp.concatenate([r[0].astype(BF16) for r in kpe_refs], axis=1)
    s = _dot_nt(ql, kc) + _dot(qp_rope, kk_t)
    m_prev = m_scr[...]
    m_new = jnp.maximum(m_prev, jnp.max(s, axis=1, keepdims=True))
    alpha = jnp.exp(m_prev - m_new)
    p = jnp.exp(s - m_new)
    l_scr[...] = alpha * l_scr[...] + jnp.sum(p, axis=1, keepdims=True)
    acc_scr[...] = alpha * acc_scr[...] + _dot(p.astype(BF16), kc)
    m_scr[...] = m_new

    @pl.when(j == pl.num_programs(1) - 1)
    def _():
        cn = cn_ref[...]
        kn = kn_ref[...]
        sn = _dot_nt(ql.astype(F32), cn) + _dot_nt(qp.astype(F32), kn)
        t_row = lax.broadcasted_iota(jnp.int32, sn.shape, 0) // MLA_HEADS
        t_col = lax.broadcasted_iota(jnp.int32, sn.shape, 1)
        sn = jnp.where(t_col <= t_row, sn, NEG_INF)
        m_prev = m_scr[...]
        m_new = jnp.maximum(m_prev, jnp.max(sn, axis=1, keepdims=True))
        alpha = jnp.exp(m_prev - m_new)
        pn = jnp.exp(sn - m_new)
        l = alpha * l_scr[...] + jnp.sum(pn, axis=1, keepdims=True)
        acc = alpha * acc_scr[...] + _dot(pn, cn)
        o_ref[...] = (acc / l).reshape(dec_seq, MLA_HEADS, KV_LORA).astype(o_ref.dtype)


def mla_sample_attention(page_table, q_lat, qp3, ckv_f32, kpe_f32, cache_ckv, cache_kpe, page_base, *, row_start, dec_seq):
    B, n_pages = page_table.shape
    page = cache_ckv.shape[1]
    P = _pick(n_pages, (16, 8, 4, 2, 1))
    nsteps = n_pages // P
    assert row_start % dec_seq == 0
    r0 = row_start // dec_seq
    H = MLA_HEADS

    def page_map(p):
        return lambda b, j, pt: (page_base + pt[b * n_pages + j * P + p], 0, 0)

    in_specs = [
        pl.BlockSpec((dec_seq, H, KV_LORA), lambda b, j, pt: (b, 0, 0)),
        pl.BlockSpec((dec_seq, H, LANES), lambda b, j, pt: (r0 + b, 0, 0)),
        pl.BlockSpec((dec_seq, KV_LORA), lambda b, j, pt: (r0 + b, 0)),
        pl.BlockSpec((dec_seq, LANES), lambda b, j, pt: (r0 + b, 0)),
    ]
    in_specs += [pl.BlockSpec((1, page, KV_LORA), page_map(p)) for p in range(P)]
    in_specs += [pl.BlockSpec((1, QK_ROPE, page), page_map(p)) for p in range(P)]
    rows = dec_seq * H
    grid_spec = pltpu.PrefetchScalarGridSpec(
        num_scalar_prefetch=1,
        grid=(B, nsteps),
        in_specs=in_specs,
        out_specs=pl.BlockSpec((dec_seq, H, KV_LORA), lambda b, j, pt: (b, 0, 0)),
        scratch_shapes=[pltpu.VMEM((rows, 1), F32), pltpu.VMEM((rows, 1), F32), pltpu.VMEM((rows, KV_LORA), F32)],
    )
    return pl.pallas_call(
        functools.partial(_paged_kernel, pages_per_step=P, dec_seq=dec_seq),
        grid_spec=grid_spec,
        out_shape=jax.ShapeDtypeStruct((B * dec_seq, H, KV_LORA), BF16),
        compiler_params=_cparams(("parallel", "arbitrary")),
    )(page_table.reshape(-1), q_lat, qp3, ckv_f32, kpe_f32, *([cache_ckv] * P), *([cache_kpe] * P))


def _top_values(s, k, with_rank=False):
    out = []
    rank = jnp.full(s.shape, float(k), F32) if with_rank else None
    for r in range(k):
        m = jnp.max(s, axis=0, keepdims=True)
        out.append(m)
        hit = s == m
        if with_rank:
            rank = jnp.where(hit, float(r), rank)
        s = jnp.where(hit, -jnp.inf, s)
    return (out, rank) if with_rank else out


def _peer_topk_kernel(q_ref, keys_ref, r1_ref, c1_ref, r2_ref, e2_ref, cnt_ref):
    K = PEER_TOPK
    for h in range(PEER_HEADS):
        qa = q_ref[:, (2 * h) * LANES:(2 * h + 1) * LANES]
        qb = q_ref[:, (2 * h + 1) * LANES:(2 * h + 2) * LANES]
        sa = _dot_nt(keys_ref[2 * h], qa)
        sb = _dot_nt(keys_ref[2 * h + 1], qb)
        ta, rank_a = _top_values(sa, K, with_rank=True)
        tb, rank_b = _top_values(sb, K, with_rank=True)
        tb_all = jnp.concatenate(tb, axis=0)
        cand_rows = [ta[a] + tb_all for a in range(K)]
        best = _top_values(jnp.concatenate(cand_rows, axis=0), K)
        tau = best[K - 1]
        z = jnp.zeros_like(best[0])
        for r in range(K):
            z = z + jnp.exp(best[r] - best[0])
        cnt = [jnp.sum((cand_rows[a] >= tau).astype(F32), axis=0, keepdims=True) for a in range(K)]
        r1_ref[h] = rank_a
        r2_ref[h] = rank_b.astype(r2_ref.dtype)
        c1_ref[h] = jnp.exp(sa - ta[0]) / z
        e2_ref[h] = jnp.exp(sb - tb[0]).astype(e2_ref.dtype)
        cnt_ref[h] = jnp.concatenate(cnt, axis=0)


def peer_topk(q, keys):
    T = q.shape[0]
    TT = _pick(T, (512, 256, 128))
    blk = pl.BlockSpec((PEER_HEADS, PEER_NKEYS, TT), lambda i: (0, 0, i))
    f32 = jax.ShapeDtypeStruct((PEER_HEADS, PEER_NKEYS, T), F32)
    b16 = jax.ShapeDtypeStruct((PEER_HEADS, PEER_NKEYS, T), BF16)
    return pl.pallas_call(
        _peer_topk_kernel,
        grid=(T // TT,),
        in_specs=[pl.BlockSpec((TT, q.shape[1]), lambda i: (i, 0)),
                  pl.BlockSpec(keys.shape, lambda i: (0, 0, 0))],
        out_specs=[blk, blk, blk, blk, pl.BlockSpec((PEER_HEADS, PEER_TOPK, TT), lambda i: (0, 0, i))],
        out_shape=[f32, f32, b16, b16, jax.ShapeDtypeStruct((PEER_HEADS, PEER_TOPK, T), F32)],
        compiler_params=_cparams(("parallel",)),
    )(q, keys)


PEER_TE = 8 * PEER_NKEYS


def _peer_dense_kernel(x_ref, u_ref, vt_ref, r1_ref, c1_ref, r2_ref, e2_ref, cnt_ref, o_ref, acc_scr, a_scr):
    j = pl.program_id(1)

    @pl.when(j == 0)
    def _():
        acc_scr[...] = jnp.zeros_like(acc_scr)

    x = x_ref[...]
    TT = x.shape[0]
    rank_id = lax.broadcasted_iota(jnp.int32, (PEER_TOPK, TT), 0).astype(F32)
    zero = jnp.zeros((), BF16)
    gps = 2
    for g in range(PEER_TE // PEER_NKEYS):
        if g % gps == 0:
            slab = slice(g * PEER_NKEYS, (g + gps) * PEER_NKEYS)
            ht = _dot_nt(u_ref[0, slab, :], x)
        gate = None
        for h in range(PEER_HEADS):
            r1 = r1_ref[h, g:g + 1, :]
            k1 = jnp.sum(jnp.where(r1 == rank_id, cnt_ref[h], 0.0), axis=0, keepdims=True)
            c1 = c1_ref[h, g:g + 1, :].astype(BF16)
            term = jnp.where(r2_ref[h] < k1.astype(BF16), e2_ref[h] * c1, zero)
            gate = term if gate is None else gate + term
        hs = ht[(g % gps) * PEER_NKEYS:(g % gps + 1) * PEER_NKEYS, :]
        a_scr[g * PEER_NKEYS:(g + 1) * PEER_NKEYS, :] = _gelu(hs).astype(BF16) * gate
    acc_scr[...] += _dot(vt_ref[0], a_scr[...])

    @pl.when(j == pl.num_programs(1) - 1)
    def _():
        o_ref[...] = acc_scr[...].T


def peer_dense(x, u, vt, layer, r1, c1, r2, e2, cnt):
    T, D = x.shape
    E = u.shape[1]
    TT = _pick(T, (512, 256, 128))
    TE = PEER_TE
    assert E % TE == 0 and E == PEER_NKEYS * PEER_NKEYS
    full = pl.BlockSpec((PEER_HEADS, PEER_NKEYS, TT), lambda i, j: (0, 0, i))
    rows8 = pl.BlockSpec((PEER_HEADS, SUBLANES, TT), lambda i, j: (0, j, i))
    return pl.pallas_call(
        _peer_dense_kernel,
        grid=(T // TT, E // TE),
        in_specs=[pl.BlockSpec((TT, D), lambda i, j: (i, 0)),
                  pl.BlockSpec((1, TE, D), lambda i, j: (layer, j, 0)),
                  pl.BlockSpec((1, D, TE), lambda i, j: (layer, 0, j)),
                  rows8, rows8, full, full,
                  pl.BlockSpec((PEER_HEADS, PEER_TOPK, TT), lambda i, j: (0, 0, i))],
        out_specs=pl.BlockSpec((TT, D), lambda i, j: (i, 0)),
        out_shape=jax.ShapeDtypeStruct((T, D), F32),
        scratch_shapes=[pltpu.VMEM((D, TT), F32), pltpu.VMEM((TE, TT), BF16)],
        compiler_params=_cparams(("parallel", "arbitrary")),
    )(x, u, vt, r1, c1, r2, e2, cnt)


def _cast_transpose_kernel(v_ref, o_ref):
    o_ref[0] = v_ref[0].T.astype(o_ref.dtype)


def cast_transpose(v):
    L, E, D = v.shape
    te = _pick(E, (512, 256, 128))
    return pl.pallas_call(
        _cast_transpose_kernel,
        grid=(L, E // te),
        in_specs=[pl.BlockSpec((1, te, D), lambda l, j: (l, j, 0))],
        out_specs=pl.BlockSpec((1, D, te), lambda l, j: (l, 0, j)),
        out_shape=jax.ShapeDtypeStruct((L, D, E), BF16),
        compiler_params=_cparams(("parallel", "parallel")),
    )(v)


def _rope_swap_cols(w):
    half = QK_ROPE // 2
    return jnp.concatenate([-w[..., half:], w[..., :half]], axis=-1)


def _prep_even(w_in, w_gate_up, w_out):
    qk = 2 * GLA_HEADS * GLA_DK
    vw = GLA_HEADS * GLA_DV
    g0 = qk + vw
    g1 = g0 + GLA_GATE_RANK
    K = w_in.shape[0]
    w_perm = jnp.concatenate(
        [w_in[:, :g0], w_in[:, g1:], w_in[:, g0:g1], jnp.zeros((K, GLOW_PAD - GLA_GATE_RANK), w_in.dtype)], axis=1)
    wg = jnp.concatenate([w_gate_up, jnp.zeros((LANES - GLA_GATE_RANK, w_gate_up.shape[1]), w_gate_up.dtype)], axis=0)
    return w_perm.astype(BF16), wg.astype(BF16), w_out[:vw].astype(BF16), w_out[vw:].astype(BF16)


def _prep_gmlp_spatial(w_sp, b_sp, seq):
    C = min(GMLP_CHUNK, seq)
    assert seq % C == 0 and GMLP_CHUNK % C == 0
    w = jnp.where(jnp.tril(jnp.ones((C, C), bool)), w_sp[:, :C, :C], 0.0)
    b = b_sp[:, :C]
    rep = GMLP_CHUNK // C
    if rep > 1:
        eye = jnp.eye(rep, dtype=w.dtype)
        w = jnp.einsum('ab,gij->gaibj', eye, w).reshape(w.shape[0], GMLP_CHUNK, GMLP_CHUNK)
        b = jnp.tile(b, (1, rep))
    return w.astype(BF16), b


def _prep_odd(w_in, w_uq, w_uk, w_uv, w_o):
    kpe_w = w_in[:, Q_LORA + KV_LORA:]
    w_in_p = jnp.concatenate([w_in, _rope_swap_cols(kpe_w)], axis=1).astype(BF16)
    w_n = w_uq[:, :, :QK_NOPE].reshape(Q_LORA, MLA_HEADS * QK_NOPE).astype(BF16)
    pe = w_uq[:, :, QK_NOPE:]
    zpad = jnp.zeros((Q_LORA, MLA_HEADS, LANES - QK_ROPE), w_uq.dtype)
    w_pe = jnp.concatenate([pe, zpad], axis=-1).reshape(Q_LORA, MLA_HEADS * LANES).astype(BF16)
    w_pe_sw = jnp.concatenate([_rope_swap_cols(pe), zpad], axis=-1).reshape(Q_LORA, MLA_HEADS * LANES).astype(BF16)
    w_uk2 = w_uk.reshape(KV_LORA, MLA_HEADS * QK_NOPE).astype(BF16)
    w_uv2 = w_uv.reshape(KV_LORA, MLA_HEADS * V_HEAD).astype(BF16)
    return w_in_p, w_n, w_pe, w_pe_sw, w_uk2, w_uv2, w_o.astype(BF16)


def _rope_tables(pos):
    inv = ROPE_THETA ** (-jnp.arange(0, QK_ROPE, 2, dtype=F32) / QK_ROPE)
    ang = pos.astype(F32)[:, None] * inv[None, :]
    cos, sin = jnp.cos(ang), jnp.sin(ang)
    z = jnp.zeros((pos.shape[0], LANES - QK_ROPE), F32)
    cs_k = jnp.concatenate([cos, cos, sin, sin], axis=1)
    cs_q = jnp.concatenate([cos, cos, z, sin, sin, z], axis=1)
    return cs_k, cs_q


def kernel(x_prompt, x_sample, state_gla, cache_ckv, cache_kpe, page_table, w_in_even, w_gate_up, b_gate, gla_norm_g, gmlp_ln_g, gmlp_ln_b, w_spatial, b_spatial, w_out_even, w_in_odd, q_norm_g, kv_norm_g, w_uq, w_uk, w_uv, w_o_odd, ln_mix_g, ln_mix_b, ln_ffn_g, ln_ffn_b, w_peer_q, peer_keys, peer_u, peer_v):
    B, L, D = x_prompt.shape
    Bs, Ls, _ = x_sample.shape
    Tp, Ts = B * L, Bs * Ls
    T = Tp + Ts
    depth = ln_mix_g.shape[0]
    n_pool, page = cache_ckv.shape[1], cache_ckv.shape[2]
    past_len = page_table.shape[1] * page
    dn_alpha = (2.0 * depth) ** 0.25

    x = jnp.concatenate([x_prompt.reshape(Tp, D), x_sample.reshape(Ts, D)], axis=0)
    xb = x.astype(BF16)

    pos = jnp.concatenate([jnp.tile(jnp.arange(L, dtype=jnp.int32), B),
                           jnp.tile(past_len + jnp.arange(Ls, dtype=jnp.int32), Bs)])
    cs_k, cs_q = _rope_tables(pos)

    peer_u_b = peer_u.astype(BF16)
    peer_vt_b = cast_transpose(peer_v)
    cache_ckv2 = cache_ckv.reshape(-1, page, KV_LORA)
    cache_kpe2 = jnp.swapaxes(cache_kpe, 2, 3).reshape(-1, QK_ROPE, page)
    zeros_state = jnp.zeros((B, GLA_HEADS, GLA_DK, GLA_DV), F32)

    gla_p, gla_s, gmlp_s, ckv_out, kpe_out = [], [], [], [], []
    for l in range(depth):
        i = l // 2
        if l % 2 == 0:
            w_in, wg, w_out_o, w_out_m = _prep_even(w_in_even[i], w_gate_up[i], w_out_even[i])
            hm = mm(xb, w_in, F32)
            bg = b_gate[i].reshape(1, -1)
            gn = gla_norm_g[i].reshape(1, -1)
            o_p, s_p = gla(hm, wg, bg, gn, zeros_state, row_start=0, batch=B, seq=L)
            o_s, s_s = gla(hm, wg, bg, gn, state_gla[i], row_start=Tp, batch=Bs, seq=Ls)
            wsp_p, bsp_p = _prep_gmlp_spatial(w_spatial[i], b_spatial[i], L)
            wsp_s, bsp_s = _prep_gmlp_spatial(w_spatial[i], b_spatial[i], Ls)
            (m_p,) = gmlp(hm, gmlp_ln_g[i], gmlp_ln_b[i], wsp_p, bsp_p, row_start=0, rows=Tp, emit_v=False)
            m_s, gv_s = gmlp(hm, gmlp_ln_g[i], gmlp_ln_b[i], wsp_s, bsp_s, row_start=Tp, rows=Ts, emit_v=True)
            y = mm2(jnp.concatenate([o_p, o_s.astype(BF16)], axis=0), w_out_o, jnp.concatenate([m_p, m_s], axis=0), w_out_m)
            gla_p.append(s_p)
            gla_s.append(s_s)
            gmlp_s.append(gv_s.reshape(Bs, Ls, GMLP_GROUPS, GMLP_DG))
        else:
            w_in, w_n, w_pe, w_pe_sw, w_uk2, w_uv2, w_o = _prep_odd(w_in_odd[i], w_uq[i], w_uk[i], w_uv[i], w_o_odd[i])
            h = mm(xb, w_in, F32)
            cq, ckv, ckv_b, kpe, kpe_b = mla_post(h, cs_k, q_norm_g[i], kv_norm_g[i])
            qn = mm(cq, w_n, BF16, scale=MLA_SCALE)
            qp = q_rope(cq, w_pe, w_pe_sw, cs_q)
            kn = mm(ckv_b, w_uk2, BF16, rows=Tp)
            vv = mm(ckv_b, w_uv2, BF16, rows=Tp)
            o_p = mla_prompt_attention(qn, qp, kn, kpe_b, vv, batch=B, seq=L)
            q_lat = head_mm(qn, w_uk2, MLA_HEADS, QK_NOPE, KV_LORA, transpose_w=True, row_start=Tp, rows=Ts)
            o_lat = mla_sample_attention(page_table, q_lat.reshape(Ts, MLA_HEADS, KV_LORA),
                                         qp.reshape(T, MLA_HEADS, LANES), ckv, kpe, cache_ckv2, cache_kpe2,
                                         i * n_pool, row_start=Tp, dec_seq=Ls)
            o_s = head_mm(o_lat.reshape(Ts, MLA_HEADS * KV_LORA), w_uv2, MLA_HEADS, KV_LORA, V_HEAD, transpose_w=False)
            y = mm(jnp.concatenate([o_p, o_s], axis=0), w_o, F32)
            ckv_out.append(ckv)
            kpe_out.append(kpe[:, :QK_ROPE])
        x, xb = residual_ln(x, y, ln_mix_g[l], ln_mix_b[l], dn_alpha)
        q = mm(xb, w_peer_q[l].astype(BF16), BF16)
        keys = peer_keys[l].reshape(PEER_HEADS * 2, PEER_NKEYS, -1).astype(BF16)
        r1, c1, r2, e2, cnt = peer_topk(q, keys)
        y = peer_dense(xb, peer_u_b, peer_vt_b, l, r1, c1, r2, e2, cnt)
        x, xb = residual_ln(x, y, ln_ffn_g[l], ln_ffn_b[l], dn_alpha)

    def split(rows_list, width):
        p = jnp.stack([r[:Tp].reshape(B, L, width) for r in rows_list])
        s = jnp.stack([r[Tp:].reshape(Bs, Ls, width) for r in rows_list])
        return p, s

    ckv_p, ckv_s = split(ckv_out, KV_LORA)
    kpe_p, kpe_s = split(kpe_out, QK_ROPE)
    return (x[:Tp].reshape(B, L, D), x[Tp:].reshape(Bs, Ls, D),
            jnp.stack(gla_p), jnp.stack(gla_s), jnp.stack(gmlp_s),
            ckv_p, kpe_p, ckv_s, kpe_s)
```

```python
import functools

import jax
import jax.numpy as jnp
from jax import lax
from jax.experimental import pallas as pl
from jax.experimental.pallas import tpu as pltpu

F32 = jnp.float32
BF16 = jnp.bfloat16

GLA_HEADS = 4
GLA_DK = 128
GLA_DV = 256
GLA_GATE_RANK = 16
GLA_TAU = 16.0
GLA_CHUNK = 64
GMLP_GROUPS = 4
GMLP_DG = 256
GMLP_CHUNK = 128
MLA_HEADS = 16
Q_LORA = 512
KV_LORA = 512
QK_NOPE = 128
QK_ROPE = 64
V_HEAD = 128
ROPE_THETA = 10000.0
MLA_SCALE = (QK_NOPE + QK_ROPE) ** -0.5
PEER_HEADS = 8
PEER_NKEYS = 128
PEER_TOPK = 16
LN_EPS = 1e-5
RMS_EPS = 1e-6
NEG_INF = -1e30

LANES = 128
SUBLANES = 8
VMEM_LIMIT_BYTES = 56 * 1024 * 1024

GLOW_PAD = 256


def _cparams(sem):
    return pltpu.CompilerParams(dimension_semantics=sem, vmem_limit_bytes=VMEM_LIMIT_BYTES)


def _pick(n, prefs):
    for p in prefs:
        if n % p == 0:
            return p
    raise ValueError(f"no tile in {prefs} divides {n}")


def _gelu(x):
    return 0.5 * x * (1.0 + jnp.tanh(0.7978845608028654 * (x + 0.044715 * (x * x * x))))


def _dot(a, b, **kw):
    return jnp.dot(a, b, preferred_element_type=F32, **kw)


def _dot_nt(a, b, **kw):
    return lax.dot_general(a, b, (((1,), (1,)), ((), ())), preferred_element_type=F32, **kw)


def _dot_tn(a, b, **kw):
    return lax.dot_general(a, b, (((0,), (0,)), ((), ())), preferred_element_type=F32, **kw)


def _mm_kernel(x_ref, w_ref, o_ref, *, scale):
    acc = _dot(x_ref[...], w_ref[...])
    if scale != 1.0:
        acc = acc * scale
    o_ref[...] = acc.astype(o_ref.dtype)


def mm(x, w, out_dtype, *, rows=None, row_start=0, scale=1.0, tn_prefs=(1792, 1024, 768, 512, 384, 256, 128)):
    M, K = x.shape
    N = w.shape[1]
    rows = M - row_start if rows is None else rows
    tm = _pick(rows, (1024, 512, 256, 128))
    assert row_start % tm == 0
    tn = _pick(N, tn_prefs)
    r0 = row_start // tm
    return pl.pallas_call(
        functools.partial(_mm_kernel, scale=scale),
        grid=(rows // tm, N // tn),
        in_specs=[pl.BlockSpec((tm, K), lambda i, j: (i + r0, 0)),
                  pl.BlockSpec((K, tn), lambda i, j: (0, j))],
        out_specs=pl.BlockSpec((tm, tn), lambda i, j: (i, j)),
        out_shape=jax.ShapeDtypeStruct((rows, N), out_dtype),
        compiler_params=_cparams(("parallel", "parallel")),
    )(x, w)


def _mm2_kernel(x1_ref, w1_ref, x2_ref, w2_ref, o_ref):
    o_ref[...] = _dot(x1_ref[...], w1_ref[...]) + _dot(x2_ref[...], w2_ref[...])


def mm2(x1, w1, x2, w2):
    M, K1 = x1.shape
    K2 = x2.shape[1]
    N = w1.shape[1]
    tm = _pick(M, (1024, 512, 256, 128))
    tn = _pick(N, (1024, 512, 256, 128))
    return pl.pallas_call(
        _mm2_kernel,
        grid=(M // tm, N // tn),
        in_specs=[pl.BlockSpec((tm, K1), lambda i, j: (i, 0)),
                  pl.BlockSpec((K1, tn), lambda i, j: (0, j)),
                  pl.BlockSpec((tm, K2), lambda i, j: (i, 0)),
                  pl.BlockSpec((K2, tn), lambda i, j: (0, j))],
        out_specs=pl.BlockSpec((tm, tn), lambda i, j: (i, j)),
        out_shape=jax.ShapeDtypeStruct((M, N), F32),
        compiler_params=_cparams(("parallel", "parallel")),
    )(x1, w1, x2, w2)


def _headmm_kernel(x_ref, w_ref, o_ref, *, transpose_w):
    if transpose_w:
        o_ref[...] = _dot_nt(x_ref[...], w_ref[...]).astype(o_ref.dtype)
    else:
        o_ref[...] = _dot(x_ref[...], w_ref[...]).astype(o_ref.dtype)


def head_mm(x, w, heads, kin, kout, *, transpose_w, row_start=0, rows=None):
    M = x.shape[0]
    rows = M - row_start if rows is None else rows
    tm = _pick(rows, (1024, 512, 256, 128))
    assert row_start % tm == 0
    r0 = row_start // tm
    wblock = (kout, kin) if transpose_w else (kin, kout)
    return pl.pallas_call(
        functools.partial(_headmm_kernel, transpose_w=transpose_w),
        grid=(rows // tm, heads),
        in_specs=[pl.BlockSpec((tm, kin), lambda i, h: (i + r0, h)),
                  pl.BlockSpec(wblock, lambda i, h: (0, h))],
        out_specs=pl.BlockSpec((tm, kout), lambda i, h: (i, h)),
        out_shape=jax.ShapeDtypeStruct((rows, heads * kout), BF16),
        compiler_params=_cparams(("parallel", "parallel")),
    )(x, w)


def _ln_kernel(x_ref, y_ref, g_ref, b_ref, of_ref, ob_ref, *, alpha):
    z = alpha * x_ref[...] + y_ref[...]
    mu = jnp.mean(z, axis=-1, keepdims=True)
    zc = z - mu
    var = jnp.mean(zc * zc, axis=-1, keepdims=True)
    out = zc * lax.rsqrt(var + LN_EPS) * g_ref[...] + b_ref[...]
    of_ref[...] = out
    ob_ref[...] = out.astype(BF16)


def residual_ln(x, y, g, b, alpha):
    T, D = x.shape
    tm = _pick(T, (256, 128))
    row = pl.BlockSpec((tm, D), lambda i: (i, 0))
    vec = pl.BlockSpec((1, D), lambda i: (0, 0))
    return pl.pallas_call(
        functools.partial(_ln_kernel, alpha=alpha),
        grid=(T // tm,),
        in_specs=[row, row, vec, vec],
        out_specs=[row, row],
        out_shape=[jax.ShapeDtypeStruct((T, D), F32), jax.ShapeDtypeStruct((T, D), BF16)],
        compiler_params=_cparams(("parallel",)),
    )(x, y, g.reshape(1, D), b.reshape(1, D))


def _gla_kernel(q_ref, k_ref, v_ref, gl_ref, r_ref, wg_ref, bg_ref, gn_ref, s0_ref,
                o_ref, s_ref, state, *, chunk, n_chunks, mxu_dtype):
    rb = pl.program_id(1)

    @pl.when(rb == 0)
    def _():
        state[...] = s0_ref[0]

    C = chunk
    row = lax.broadcasted_iota(jnp.int32, (C, C), 0)
    col = lax.broadcasted_iota(jnp.int32, (C, C), 1)
    causal = row >= col
    tri = causal.astype(F32)
    ones_cv = jnp.ones((C, GLA_DV), F32)
    hi = lax.Precision.HIGHEST
    md = mxu_dtype

    for c in range(n_chunks):
        sl = slice(c * C, (c + 1) * C)
        glow = gl_ref[sl, :].astype(BF16)
        outs, states = [], []
        for h in range(GLA_HEADS):
            kcols = slice(h * GLA_DK, (h + 1) * GLA_DK)
            vcols = slice(h * GLA_DV, (h + 1) * GLA_DV)
            qi = q_ref[sl, kcols] * (GLA_DK ** -0.5)
            ki = k_ref[sl, kcols]
            vi = v_ref[sl, vcols]
            z = _dot(glow, wg_ref[:, kcols]) + bg_ref[:, kcols]
            gi = (jnp.minimum(z, 0.0) - jnp.log(1.0 + jnp.exp(-jnp.abs(z)))) * (1.0 / GLA_TAU)
            b = _dot(tri, gi, precision=hi)
            b_last = b[C - 1:C, :]
            q_g = qi * jnp.exp(b)
            k_g = ki * jnp.exp(-b)
            att = jnp.where(causal, _dot_nt(q_g.astype(md), k_g.astype(md)), 0.0)
            s_prev = state[h]
            o = _dot(att.astype(md), vi.astype(md)) + _dot(q_g.astype(md), s_prev.astype(md))
            k_dec = ki * jnp.exp(b_last - b)
            decay = _dot_tn(gi, ones_cv, precision=hi)
            states.append(jnp.exp(decay) * s_prev + _dot_tn(k_dec.astype(md), vi.astype(md)))
            ms = jnp.mean(o * o, axis=-1, keepdims=True)
            on = o * lax.rsqrt(ms + RMS_EPS) * gn_ref[...]
            rr = r_ref[sl, vcols]
            outs.append((on * (rr * jax.nn.sigmoid(rr))).astype(o_ref.dtype))
        for h in range(GLA_HEADS):
            state[h] = states[h]
            o_ref[sl, h * GLA_DV:(h + 1) * GLA_DV] = outs[h]

    @pl.when(rb == pl.num_programs(1) - 1)
    def _():
        s_ref[0] = state[...]


def gla(hm, wg, bg, gn, s0, *, row_start, batch, seq):
    C = min(GLA_CHUNK, seq)
    assert seq % C == 0
    n_chunks = _pick(seq // C, (4, 2, 1))
    R = C * n_chunks
    nrb = seq // R
    assert row_start % R == 0
    r0 = row_start // R
    H = GLA_HEADS
    QW, VW = H * GLA_DK, H * GLA_DV
    glow_blk = (2 * QW + 2 * VW + 2 * GMLP_GROUPS * GMLP_DG) // LANES
    rowmap = lambda blk: (lambda b, rb: (r0 + b * nrb + rb, blk))
    mxu_dtype = BF16 if C >= 16 else F32
    out_dtype = BF16 if R >= 16 else F32
    kernel = functools.partial(_gla_kernel, chunk=C, n_chunks=n_chunks, mxu_dtype=mxu_dtype)
    return pl.pallas_call(
        kernel,
        grid=(batch, nrb),
        in_specs=[
            pl.BlockSpec((R, QW), rowmap(0)),
            pl.BlockSpec((R, QW), rowmap(1)),
            pl.BlockSpec((R, VW), rowmap(2 * QW // VW)),
            pl.BlockSpec((R, LANES), rowmap(glow_blk)),
            pl.BlockSpec((R, VW), rowmap(2 * QW // VW + 1)),
            pl.BlockSpec((LANES, QW), lambda b, rb: (0, 0)),
            pl.BlockSpec((1, QW), lambda b, rb: (0, 0)),
            pl.BlockSpec((1, GLA_DV), lambda b, rb: (0, 0)),
            pl.BlockSpec((1, H, GLA_DK, GLA_DV), lambda b, rb: (b, 0, 0, 0)),
        ],
        out_specs=[
            pl.BlockSpec((R, VW), lambda b, rb: (b * nrb + rb, 0)),
            pl.BlockSpec((1, H, GLA_DK, GLA_DV), lambda b, rb: (b, 0, 0, 0)),
        ],
        out_shape=[jax.ShapeDtypeStruct((batch * seq, VW), out_dtype),
                   jax.ShapeDtypeStruct((batch, H, GLA_DK, GLA_DV), F32)],
        scratch_shapes=[pltpu.VMEM((H, GLA_DK, GLA_DV), F32)],
        compiler_params=_cparams(("parallel", "arbitrary")),
    )(hm, hm, hm, hm, hm, wg, bg, gn, s0)


def _gmlp_kernel(gu_ref, gv_ref, lng_ref, lnb_ref, w_ref, b_ref, m_ref, *gvo_ref):
    u = _gelu(gu_ref[...])
    v = _gelu(gv_ref[...])
    mu = jnp.mean(v, axis=-1, keepdims=True)
    vc = v - mu
    var = jnp.mean(vc * vc, axis=-1, keepdims=True)
    vn = vc * lax.rsqrt(var + LN_EPS) * lng_ref[0] + lnb_ref[0]
    s = _dot(w_ref[0], vn.astype(BF16)) + b_ref[0]
    m_ref[...] = (u * s).astype(m_ref.dtype)
    if gvo_ref:
        gvo_ref[0][...] = vn


def gmlp(hm, lng, lnb, w_eff, b_eff, *, row_start, rows, emit_v):
    G = GMLP_GROUPS
    R = GMLP_CHUNK
    assert rows % R == 0 and row_start % R == 0
    r0 = row_start // R
    ublk = (2 * GLA_HEADS * GLA_DK + 2 * GLA_HEADS * GLA_DV) // GMLP_DG
    vblk = ublk + G
    out_block = pl.BlockSpec((R, GMLP_DG), lambda i, g: (i, g))
    out_specs = [out_block]
    out_shape = [jax.ShapeDtypeStruct((rows, G * GMLP_DG), BF16)]
    if emit_v:
        out_specs.append(out_block)
        out_shape.append(jax.ShapeDtypeStruct((rows, G * GMLP_DG), F32))
    return pl.pallas_call(
        _gmlp_kernel,
        grid=(rows // R, G),
        in_specs=[
            pl.BlockSpec((R, GMLP_DG), lambda i, g: (r0 + i, ublk + g)),
            pl.BlockSpec((R, GMLP_DG), lambda i, g: (r0 + i, vblk + g)),
            pl.BlockSpec((1, 1, GMLP_DG), lambda i, g: (g, 0, 0)),
            pl.BlockSpec((1, 1, GMLP_DG), lambda i, g: (g, 0, 0)),
            pl.BlockSpec((1, R, R), lambda i, g: (g, 0, 0)),
            pl.BlockSpec((1, R, 1), lambda i, g: (g, 0, 0)),
        ],
        out_specs=out_specs,
        out_shape=out_shape,
        compiler_params=_cparams(("parallel", "parallel")),
    )(hm, hm, lng.reshape(G, 1, GMLP_DG), lnb.reshape(G, 1, GMLP_DG), w_eff, b_eff.reshape(G, R, 1))


def _mla_post_kernel(h_ref, cs_ref, qg_ref, kg_ref, cq_ref, ckv_ref, ckvb_ref, kpe_ref, kpeb_ref):
    cq = h_ref[:, :Q_LORA]
    ckv = h_ref[:, Q_LORA:Q_LORA + KV_LORA]
    cqn = cq * lax.rsqrt(jnp.mean(cq * cq, axis=-1, keepdims=True) + RMS_EPS) * qg_ref[...]
    ckvn = ckv * lax.rsqrt(jnp.mean(ckv * ckv, axis=-1, keepdims=True) + RMS_EPS) * kg_ref[...]
    cq_ref[...] = cqn.astype(BF16)
    ckv_ref[...] = ckvn
    ckvb_ref[...] = ckvn.astype(BF16)
    t = h_ref[:, Q_LORA + KV_LORA:] * cs_ref[...]
    rot = t + pltpu.roll(t, QK_ROPE, 1)
    lane = lax.broadcasted_iota(jnp.int32, rot.shape, 1)
    rot = jnp.where(lane < QK_ROPE, rot, 0.0)
    kpe_ref[...] = rot
    kpeb_ref[...] = rot.astype(BF16)


def mla_post(h, cs, q_norm_g, kv_norm_g):
    T = h.shape[0]
    tm = _pick(T, (512, 256, 128))
    W = h.shape[1]
    return pl.pallas_call(
        _mla_post_kernel,
        grid=(T // tm,),
        in_specs=[pl.BlockSpec((tm, W), lambda i: (i, 0)),
                  pl.BlockSpec((tm, LANES), lambda i: (i, 0)),
                  pl.BlockSpec((1, Q_LORA), lambda i: (0, 0)),
                  pl.BlockSpec((1, KV_LORA), lambda i: (0, 0))],
        out_specs=[pl.BlockSpec((tm, Q_LORA), lambda i: (i, 0)),
                   pl.BlockSpec((tm, KV_LORA), lambda i: (i, 0)),
                   pl.BlockSpec((tm, KV_LORA), lambda i: (i, 0)),
                   pl.BlockSpec((tm, LANES), lambda i: (i, 0)),
                   pl.BlockSpec((tm, LANES), lambda i: (i, 0))],
        out_shape=[jax.ShapeDtypeStruct((T, Q_LORA), BF16),
                   jax.ShapeDtypeStruct((T, KV_LORA), F32),
                   jax.ShapeDtypeStruct((T, KV_LORA), BF16),
                   jax.ShapeDtypeStruct((T, LANES), F32),
                   jax.ShapeDtypeStruct((T, LANES), BF16)],
        compiler_params=_cparams(("parallel",)),
    )(h, cs, q_norm_g.reshape(1, Q_LORA), kv_norm_g.reshape(1, KV_LORA))


def _qpe_kernel(cq_ref, w_ref, wsw_ref, cs_ref, o_ref):
    a = _dot(cq_ref[...], w_ref[...])
    b = _dot(cq_ref[...], wsw_ref[...])
    cos = cs_ref[:, :LANES]
    sin = cs_ref[:, LANES:]
    for h in range(MLA_HEADS):
        sl = slice(h * LANES, (h + 1) * LANES)
        o_ref[:, sl] = ((a[:, sl] * cos + b[:, sl] * sin) * MLA_SCALE).astype(o_ref.dtype)


def q_rope(cq, w_pe, w_pe_sw, cs2):
    T, K = cq.shape
    N = w_pe.shape[1]
    tm = _pick(T, (512, 256, 128))
    return pl.pallas_call(
        _qpe_kernel,
        grid=(T // tm,),
        in_specs=[pl.BlockSpec((tm, K), lambda i: (i, 0)),
                  pl.BlockSpec((K, N), lambda i: (0, 0)),
                  pl.BlockSpec((K, N), lambda i: (0, 0)),
                  pl.BlockSpec((tm, 2 * LANES), lambda i: (i, 0))],
        out_specs=pl.BlockSpec((tm, N), lambda i: (i, 0)),
        out_shape=jax.ShapeDtypeStruct((T, N), BF16),
        compiler_params=_cparams(("parallel",)),
    )(cq, w_pe, w_pe_sw, cs2)


def _flash_kernel(qn_ref, qp_ref, kn_ref, kp_ref, vt_ref, o_ref, m_scr, l_scr, acc_scr, *, tq):
    qi = pl.program_id(2)
    q = jnp.concatenate([qn_ref[...], qp_ref[...]], axis=1)
    m_scr[...] = jnp.full_like(m_scr, NEG_INF)
    l_scr[...] = jnp.zeros_like(l_scr)
    acc_scr[...] = jnp.zeros_like(acc_scr)

    def block(ki, masked):
        start = pl.multiple_of(ki * tq, tq)
        k = jnp.concatenate([kn_ref[pl.ds(start, tq), :], kp_ref[pl.ds(start, tq), :]], axis=1)
        st = _dot_nt(k, q)
        if masked:
            key = lax.broadcasted_iota(jnp.int32, st.shape, 0)
            qry = lax.broadcasted_iota(jnp.int32, st.shape, 1)
            st = jnp.where(key <= qry, st, NEG_INF)
        m_prev = m_scr[...]
        m_new = jnp.maximum(m_prev, jnp.max(st, axis=0, keepdims=True))
        alpha = jnp.exp(m_prev - m_new)
        p = jnp.exp(st - m_new)
        l_scr[...] = alpha * l_scr[...] + jnp.sum(p, axis=0, keepdims=True)
        acc_scr[...] = alpha * acc_scr[...] + _dot(vt_ref[ki], p.astype(BF16))
        m_scr[...] = m_new

    def body(ki, carry):
        block(ki, False)
        return carry

    lax.fori_loop(0, qi, body, 0)
    block(qi, True)
    o_ref[...] = (acc_scr[...] / l_scr[...]).T.astype(o_ref.dtype)


def _vt_kernel(w_ref, c_ref, o_ref):
    o_ref[0] = _dot_nt(w_ref[...], c_ref[...]).astype(o_ref.dtype)


def value_blocks_t(ckv, w_uv_t, *, rows, tk):
    K = ckv.shape[1]
    N = w_uv_t.shape[0]
    return pl.pallas_call(
        _vt_kernel,
        grid=(rows // tk,),
        in_specs=[pl.BlockSpec((N, K), lambda i: (0, 0)),
                  pl.BlockSpec((tk, K), lambda i: (i, 0))],
        out_specs=pl.BlockSpec((1, N, tk), lambda i: (i, 0, 0)),
        out_shape=jax.ShapeDtypeStruct((rows // tk, N, tk), BF16),
        compiler_params=_cparams(("parallel",)),
    )(w_uv_t, ckv)


def mla_prompt_attention(qn, qp, kn, kp, vt, *, batch, seq, tq):
    nq = seq // tq
    H = MLA_HEADS
    return pl.pallas_call(
        functools.partial(_flash_kernel, tq=tq),
        grid=(batch, H, nq),
        in_specs=[pl.BlockSpec((tq, LANES), lambda b, h, qi: (b * nq + qi, h)),
                  pl.BlockSpec((tq, LANES), lambda b, h, qi: (b * nq + qi, h)),
                  pl.BlockSpec((seq, LANES), lambda b, h, qi: (b, h)),
                  pl.BlockSpec((seq, LANES), lambda b, h, qi: (b, 0)),
                  pl.BlockSpec((nq, V_HEAD, tq), lambda b, h, qi: (b, h, 0))],
        out_specs=pl.BlockSpec((tq, LANES), lambda b, h, qi: (b * nq + qi, h)),
        out_shape=jax.ShapeDtypeStruct((batch * seq, H * V_HEAD), BF16),
        scratch_shapes=[pltpu.VMEM((1, tq), F32), pltpu.VMEM((1, tq), F32), pltpu.VMEM((V_HEAD, tq), F32)],
        compiler_params=_cparams(("parallel", "parallel", "arbitrary")),
    )(qn, qp, kn, kp, vt)


def _paged_kernel(pt_ref, ql_ref, qp_ref, cn_ref, kn_ref, *rest, pages_per_step, dec_seq):
    P = pages_per_step
    ckv_refs = rest[:P]
    kpe_refs = rest[P:2 * P]
    o_ref = rest[2 * P]
    m_scr, l_scr, acc_scr = rest[2 * P + 1:]
    j = pl.program_id(1)
    rows = dec_seq * MLA_HEADS

    @pl.when(j == 0)
    def _():
        m_scr[...] = jnp.full_like(m_scr, NEG_INF)
        l_scr[...] = jnp.zeros_like(l_scr)
        acc_scr[...] = jnp.zeros_like(acc_scr)

    ql = ql_ref[...].reshape(rows, KV_LORA)
    qp = qp_ref[...].reshape(rows, LANES)
    qp_rope = qp[:, :QK_ROPE]

    kc = jnp.concatenate([r[0].astype(BF16) for r in ckv_refs], axis=0)
    kk_t = jnp.concatenate([r[0].astype(BF16) for r in kpe_refs], axis=1)
    s = _dot_nt(ql, kc) + _dot(qp_rope, kk_t)
    m_prev = m_scr[...]
    m_new = jnp.maximum(m_prev, jnp.max(s, axis=1, keepdims=True))
    alpha = jnp.exp(m_prev - m_new)
    p = jnp.exp(s - m_new)
    l_scr[...] = alpha * l_scr[...] + jnp.sum(p, axis=1, keepdims=True)
    acc_scr[...] = alpha * acc_scr[...] + _dot(p.astype(BF16), kc)
    m_scr[...] = m_new

    @pl.when(j == pl.num_programs(1) - 1)
    def _():
        cn = cn_ref[...]
        kn = kn_ref[...]
        sn = _dot_nt(ql.astype(F32), cn) + _dot_nt(qp.astype(F32), kn)
        t_row = lax.broadcasted_iota(jnp.int32, sn.shape, 0) // MLA_HEADS
        t_col = lax.broadcasted_iota(jnp.int32, sn.shape, 1)
        sn = jnp.where(t_col <= t_row, sn, NEG_INF)
        m_prev = m_scr[...]
        m_new = jnp.maximum(m_prev, jnp.max(sn, axis=1, keepdims=True))
        alpha = jnp.exp(m_prev - m_new)
        pn = jnp.exp(sn - m_new)
        l = alpha * l_scr[...] + jnp.sum(pn, axis=1, keepdims=True)
        acc = alpha * acc_scr[...] + _dot(pn, cn)
        o_ref[...] = (acc / l).reshape(dec_seq, MLA_HEADS, KV_LORA).astype(o_ref.dtype)


def mla_sample_attention(page_table, q_lat, qp3, ckv_f32, kpe_f32, cache_ckv, cache_kpe, page_base, *, row_start, dec_seq):
    B, n_pages = page_table.shape
    page = cache_ckv.shape[1]
    P = _pick(n_pages, (16, 8, 4, 2, 1))
    nsteps = n_pages // P
    assert row_start % dec_seq == 0
    r0 = row_start // dec_seq
    H = MLA_HEADS

    def page_map(p):
        return lambda b, j, pt: (page_base + pt[b * n_pages + j * P + p], 0, 0)

    in_specs = [
        pl.BlockSpec((dec_seq, H, KV_LORA), lambda b, j, pt: (b, 0, 0)),
        pl.BlockSpec((dec_seq, H, LANES), lambda b, j, pt: (r0 + b, 0, 0)),
        pl.BlockSpec((dec_seq, KV_LORA), lambda b, j, pt: (r0 + b, 0)),
        pl.BlockSpec((dec_seq, LANES), lambda b, j, pt: (r0 + b, 0)),
    ]
    in_specs += [pl.BlockSpec((1, page, KV_LORA), page_map(p)) for p in range(P)]
    in_specs += [pl.BlockSpec((1, QK_ROPE, page), page_map(p)) for p in range(P)]
    rows = dec_seq * H
    grid_spec = pltpu.PrefetchScalarGridSpec(
        num_scalar_prefetch=1,
        grid=(B, nsteps),
        in_specs=in_specs,
        out_specs=pl.BlockSpec((dec_seq, H, KV_LORA), lambda b, j, pt: (b, 0, 0)),
        scratch_shapes=[pltpu.VMEM((rows, 1), F32), pltpu.VMEM((rows, 1), F32), pltpu.VMEM((rows, KV_LORA), F32)],
    )
    return pl.pallas_call(
        functools.partial(_paged_kernel, pages_per_step=P, dec_seq=dec_seq),
        grid_spec=grid_spec,
        out_shape=jax.ShapeDtypeStruct((B * dec_seq, H, KV_LORA), BF16),
        compiler_params=_cparams(("parallel", "arbitrary")),
    )(page_table.reshape(-1), q_lat, qp3, ckv_f32, kpe_f32, *([cache_ckv] * P), *([cache_kpe] * P))


def _top_values(s, k, with_rank=False):
    out = []
    rank = jnp.full(s.shape, float(k), F32) if with_rank else None
    for r in range(k):
        m = jnp.max(s, axis=0, keepdims=True)
        out.append(m)
        hit = s == m
        if with_rank:
            rank = jnp.where(hit, float(r), rank)
        s = jnp.where(hit, -jnp.inf, s)
    return (out, rank) if with_rank else out


def _peer_topk_kernel(q_ref, keys_ref, r1_ref, c1_ref, r2_ref, e2_ref, cnt_ref):
    K = PEER_TOPK
    for h in range(PEER_HEADS):
        qa = q_ref[:, (2 * h) * LANES:(2 * h + 1) * LANES]
        qb = q_ref[:, (2 * h + 1) * LANES:(2 * h + 2) * LANES]
        sa = _dot_nt(keys_ref[2 * h], qa)
        sb = _dot_nt(keys_ref[2 * h + 1], qb)
        ta, rank_a = _top_values(sa, K, with_rank=True)
        tb, rank_b = _top_values(sb, K, with_rank=True)
        tb_all = jnp.concatenate(tb, axis=0)
        cand_rows = [ta[a] + tb_all for a in range(K)]
        best = _top_values(jnp.concatenate(cand_rows, axis=0), K)
        tau = best[K - 1]
        z = jnp.zeros_like(best[0])
        for r in range(K):
            z = z + jnp.exp(best[r] - best[0])
        cnt = [jnp.sum((cand_rows[a] >= tau).astype(F32), axis=0, keepdims=True) for a in range(K)]
        r1_ref[h] = rank_a
        r2_ref[h] = rank_b.astype(r2_ref.dtype)
        c1_ref[h] = jnp.exp(sa - ta[0]) / z
        e2_ref[h] = jnp.exp(sb - tb[0]).astype(e2_ref.dtype)
        cnt_ref[h] = jnp.concatenate(cnt, axis=0)


def peer_topk(q, keys):
    T = q.shape[0]
    TT = _pick(T, (512, 256, 128))
    blk = pl.BlockSpec((PEER_HEADS, PEER_NKEYS, TT), lambda i: (0, 0, i))
    f32 = jax.ShapeDtypeStruct((PEER_HEADS, PEER_NKEYS, T), F32)
    b16 = jax.ShapeDtypeStruct((PEER_HEADS, PEER_NKEYS, T), BF16)
    return pl.pallas_call(
        _peer_topk_kernel,
        grid=(T // TT,),
        in_specs=[pl.BlockSpec((TT, q.shape[1]), lambda i: (i, 0)),
                  pl.BlockSpec(keys.shape, lambda i: (0, 0, 0))],
        out_specs=[blk, blk, blk, blk, pl.BlockSpec((PEER_HEADS, PEER_TOPK, TT), lambda i: (0, 0, i))],
        out_shape=[f32, f32, b16, b16, jax.ShapeDtypeStruct((PEER_HEADS, PEER_TOPK, T), F32)],
        compiler_params=_cparams(("parallel",)),
    )(q, keys)


PEER_TE = 8 * PEER_NKEYS


def _peer_dense_kernel(x_ref, u_ref, vt_ref, r1_ref, c1_ref, r2_ref, e2_ref, cnt_ref, o_ref, acc_scr, a_scr):
    j = pl.program_id(1)

    @pl.when(j == 0)
    def _():
        acc_scr[...] = jnp.zeros_like(acc_scr)

    x = x_ref[...]
    TT = x.shape[0]
    rank_id = lax.broadcasted_iota(jnp.int32, (PEER_TOPK, TT), 0).astype(F32)
    zero = jnp.zeros((), BF16)
    gps = 2
    for g in range(PEER_TE // PEER_NKEYS):
        if g % gps == 0:
            slab = slice(g * PEER_NKEYS, (g + gps) * PEER_NKEYS)
            ht = _dot_nt(u_ref[0, slab, :], x)
        gate = None
        for h in range(PEER_HEADS):
            r1 = r1_ref[h, g:g + 1, :]
            k1 = jnp.sum(jnp.where(r1 == rank_id, cnt_ref[h], 0.0), axis=0, keepdims=True)
            c1 = c1_ref[h, g:g + 1, :].astype(BF16)
            term = jnp.where(r2_ref[h] < k1.astype(BF16), e2_ref[h] * c1, zero)
            gate = term if gate is None else gate + term
        hs = ht[(g % gps) * PEER_NKEYS:(g % gps + 1) * PEER_NKEYS, :]
        a_scr[g * PEER_NKEYS:(g + 1) * PEER_NKEYS, :] = _gelu(hs).astype(BF16) * gate
    acc_scr[...] += _dot(vt_ref[0], a_scr[...])

    @pl.when(j == pl.num_programs(1) - 1)
    def _():
        o_ref[...] = acc_scr[...].T


def peer_dense(x, u, vt, layer, r1, c1, r2, e2, cnt):
    T, D = x.shape
    E = u.shape[1]
    TT = _pick(T, (512, 256, 128))
    TE = PEER_TE
    assert E % TE == 0 and E == PEER_NKEYS * PEER_NKEYS
    full = pl.BlockSpec((PEER_HEADS, PEER_NKEYS, TT), lambda i, j: (0, 0, i))
    rows8 = pl.BlockSpec((PEER_HEADS, SUBLANES, TT), lambda i, j: (0, j, i))
    return pl.pallas_call(
        _peer_dense_kernel,
        grid=(T // TT, E // TE),
        in_specs=[pl.BlockSpec((TT, D), lambda i, j: (i, 0)),
                  pl.BlockSpec((1, TE, D), lambda i, j: (layer, j, 0)),
                  pl.BlockSpec((1, D, TE), lambda i, j: (layer, 0, j)),
                  rows8, rows8, full, full,
                  pl.BlockSpec((PEER_HEADS, PEER_TOPK, TT), lambda i, j: (0, 0, i))],
        out_specs=pl.BlockSpec((TT, D), lambda i, j: (i, 0)),
        out_shape=jax.ShapeDtypeStruct((T, D), F32),
        scratch_shapes=[pltpu.VMEM((D, TT), F32), pltpu.VMEM((TE, TT), BF16)],
        compiler_params=_cparams(("parallel", "arbitrary")),
    )(x, u, vt, r1, c1, r2, e2, cnt)


def _cast_transpose_kernel(v_ref, o_ref):
    o_ref[0] = v_ref[0].T.astype(o_ref.dtype)


def cast_transpose(v):
    L, E, D = v.shape
    te = _pick(E, (512, 256, 128))
    return pl.pallas_call(
        _cast_transpose_kernel,
        grid=(L, E // te),
        in_specs=[pl.BlockSpec((1, te, D), lambda l, j: (l, j, 0))],
        out_specs=pl.BlockSpec((1, D, te), lambda l, j: (l, 0, j)),
        out_shape=jax.ShapeDtypeStruct((L, D, E), BF16),
        compiler_params=_cparams(("parallel", "parallel")),
    )(v)


def _rope_swap_cols(w):
    half = QK_ROPE // 2
    return jnp.concatenate([-w[..., half:], w[..., :half]], axis=-1)


def _prep_even(w_in, w_gate_up, w_out):
    qk = 2 * GLA_HEADS * GLA_DK
    vw = GLA_HEADS * GLA_DV
    g0 = qk + vw
    g1 = g0 + GLA_GATE_RANK
    K = w_in.shape[0]
    w_perm = jnp.concatenate(
        [w_in[:, :g0], w_in[:, g1:], w_in[:, g0:g1], jnp.zeros((K, GLOW_PAD - GLA_GATE_RANK), w_in.dtype)], axis=1)
    wg = jnp.concatenate([w_gate_up, jnp.zeros((LANES - GLA_GATE_RANK, w_gate_up.shape[1]), w_gate_up.dtype)], axis=0)
    return w_perm.astype(BF16), wg.astype(BF16), w_out[:vw].astype(BF16), w_out[vw:].astype(BF16)


def _prep_gmlp_spatial(w_sp, b_sp, seq):
    C = min(GMLP_CHUNK, seq)
    assert seq % C == 0 and GMLP_CHUNK % C == 0
    w = jnp.where(jnp.tril(jnp.ones((C, C), bool)), w_sp[:, :C, :C], 0.0)
    b = b_sp[:, :C]
    rep = GMLP_CHUNK // C
    if rep > 1:
        eye = jnp.eye(rep, dtype=w.dtype)
        w = jnp.einsum('ab,gij->gaibj', eye, w).reshape(w.shape[0], GMLP_CHUNK, GMLP_CHUNK)
        b = jnp.tile(b, (1, rep))
    return w.astype(BF16), b


def _prep_odd(w_in, w_uq, w_uk, w_uv, w_o):
    kpe_w = w_in[:, Q_LORA + KV_LORA:]
    w_in_p = jnp.concatenate([w_in, _rope_swap_cols(kpe_w)], axis=1).astype(BF16)
    w_n = w_uq[:, :, :QK_NOPE].reshape(Q_LORA, MLA_HEADS * QK_NOPE).astype(BF16)
    pe = w_uq[:, :, QK_NOPE:]
    zpad = jnp.zeros((Q_LORA, MLA_HEADS, LANES - QK_ROPE), w_uq.dtype)
    w_pe = jnp.concatenate([pe, zpad], axis=-1).reshape(Q_LORA, MLA_HEADS * LANES).astype(BF16)
    w_pe_sw = jnp.concatenate([_rope_swap_cols(pe), zpad], axis=-1).reshape(Q_LORA, MLA_HEADS * LANES).astype(BF16)
    w_uk2 = w_uk.reshape(KV_LORA, MLA_HEADS * QK_NOPE).astype(BF16)
    w_uv2 = w_uv.reshape(KV_LORA, MLA_HEADS * V_HEAD).astype(BF16)
    return w_in_p, w_n, w_pe, w_pe_sw, w_uk2, w_uv2, w_o.astype(BF16)


def _rope_tables(pos):
    inv = ROPE_THETA ** (-jnp.arange(0, QK_ROPE, 2, dtype=F32) / QK_ROPE)
    ang = pos.astype(F32)[:, None] * inv[None, :]
    cos, sin = jnp.cos(ang), jnp.sin(ang)
    z = jnp.zeros((pos.shape[0], LANES - QK_ROPE), F32)
    cs_k = jnp.concatenate([cos, cos, sin, sin], axis=1)
    cs_q = jnp.concatenate([cos, cos, z, sin, sin, z], axis=1)
    return cs_k, cs_q


def kernel(x_prompt, x_sample, state_gla, cache_ckv, cache_kpe, page_table, w_in_even, w_gate_up, b_gate, gla_norm_g, gmlp_ln_g, gmlp_ln_b, w_spatial, b_spatial, w_out_even, w_in_odd, q_norm_g, kv_norm_g, w_uq, w_uk, w_uv, w_o_odd, ln_mix_g, ln_mix_b, ln_ffn_g, ln_ffn_b, w_peer_q, peer_keys, peer_u, peer_v):
    B, L, D = x_prompt.shape
    Bs, Ls, _ = x_sample.shape
    Tp, Ts = B * L, Bs * Ls
    T = Tp + Ts
    depth = ln_mix_g.shape[0]
    n_pool, page = cache_ckv.shape[1], cache_ckv.shape[2]
    past_len = page_table.shape[1] * page
    dn_alpha = (2.0 * depth) ** 0.25

    x = jnp.concatenate([x_prompt.reshape(Tp, D), x_sample.reshape(Ts, D)], axis=0)
    xb = x.astype(BF16)

    pos = jnp.concatenate([jnp.tile(jnp.arange(L, dtype=jnp.int32), B),
                           jnp.tile(past_len + jnp.arange(Ls, dtype=jnp.int32), Bs)])
    cs_k, cs_q = _rope_tables(pos)

    peer_u_b = peer_u.astype(BF16)
    peer_vt_b = cast_transpose(peer_v)
    cache_ckv2 = cache_ckv.reshape(-1, page, KV_LORA)
    cache_kpe2 = jnp.swapaxes(cache_kpe, 2, 3).reshape(-1, QK_ROPE, page)
    zeros_state = jnp.zeros((B, GLA_HEADS, GLA_DK, GLA_DV), F32)

    gla_p, gla_s, gmlp_s, ckv_out, kpe_out = [], [], [], [], []
    for l in range(depth):
        i = l // 2
        if l % 2 == 0:
            w_in, wg, w_out_o, w_out_m = _prep_even(w_in_even[i], w_gate_up[i], w_out_even[i])
            hm = mm(xb, w_in, F32)
            bg = b_gate[i].reshape(1, -1)
            gn = gla_norm_g[i].reshape(1, -1)
            o_p, s_p = gla(hm, wg, bg, gn, zeros_state, row_start=0, batch=B, seq=L)
            o_s, s_s = gla(hm, wg, bg, gn, state_gla[i], row_start=Tp, batch=Bs, seq=Ls)
            wsp_p, bsp_p = _prep_gmlp_spatial(w_spatial[i], b_spatial[i], L)
            wsp_s, bsp_s = _prep_gmlp_spatial(w_spatial[i], b_spatial[i], Ls)
            (m_p,) = gmlp(hm, gmlp_ln_g[i], gmlp_ln_b[i], wsp_p, bsp_p, row_start=0, rows=Tp, emit_v=False)
            m_s, gv_s = gmlp(hm, gmlp_ln_g[i], gmlp_ln_b[i], wsp_s, bsp_s, row_start=Tp, rows=Ts, emit_v=True)
            y = mm2(jnp.concatenate([o_p, o_s.astype(BF16)], axis=0), w_out_o, jnp.concatenate([m_p, m_s], axis=0), w_out_m)
            gla_p.append(s_p)
            gla_s.append(s_s)
            gmlp_s.append(gv_s.reshape(Bs, Ls, GMLP_GROUPS, GMLP_DG))
        else:
            w_in, w_n, w_pe, w_pe_sw, w_uk2, w_uv2, w_o = _prep_odd(w_in_odd[i], w_uq[i], w_uk[i], w_uv[i], w_o_odd[i])
            h = mm(xb, w_in, F32)
            cq, ckv, ckv_b, kpe, kpe_b = mla_post(h, cs_k, q_norm_g[i], kv_norm_g[i])
            qn = mm(cq, w_n, BF16, scale=MLA_SCALE)
            qp = q_rope(cq, w_pe, w_pe_sw, cs_q)
            kn = mm(ckv_b, w_uk2, BF16, rows=Tp)
            tq = _pick(L, (512, 256, 128))
            vt = value_blocks_t(ckv_b, w_uv2.T, rows=Tp, tk=tq)
            o_p = mla_prompt_attention(qn, qp, kn, kpe_b, vt, batch=B, seq=L, tq=tq)
            q_lat = head_mm(qn, w_uk2, MLA_HEADS, QK_NOPE, KV_LORA, transpose_w=True, row_start=Tp, rows=Ts)
            o_lat = mla_sample_attention(page_table, q_lat.reshape(Ts, MLA_HEADS, KV_LORA),
                                         qp.reshape(T, MLA_HEADS, LANES), ckv, kpe, cache_ckv2, cache_kpe2,
                                         i * n_pool, row_start=Tp, dec_seq=Ls)
            o_s = head_mm(o_lat.reshape(Ts, MLA_HEADS * KV_LORA), w_uv2, MLA_HEADS, KV_LORA, V_HEAD, transpose_w=False)
            y = mm(jnp.concatenate([o_p, o_s], axis=0), w_o, F32)
            ckv_out.append(ckv)
            kpe_out.append(kpe[:, :QK_ROPE])
        x, xb = residual_ln(x, y, ln_mix_g[l], ln_mix_b[l], dn_alpha)
        q = mm(xb, w_peer_q[l].astype(BF16), BF16)
        keys = peer_keys[l].reshape(PEER_HEADS * 2, PEER_NKEYS, -1).astype(BF16)
        r1, c1, r2, e2, cnt = peer_topk(q, keys)
        y = peer_dense(xb, peer_u_b, peer_vt_b, l, r1, c1, r2, e2, cnt)
        x, xb = residual_ln(x, y, ln_ffn_g[l], ln_ffn_b[l], dn_alpha)

    def split(rows_list, width):
        p = jnp.stack([r[:Tp].reshape(B, L, width) for r in rows_list])
        s = jnp.stack([r[Tp:].reshape(Bs, Ls, width) for r in rows_list])
        return p, s

    ckv_p, ckv_s = split(ckv_out, KV_LORA)
    kpe_p, kpe_s = split(kpe_out, QK_ROPE)
    return (x[:Tp].reshape(B, L, D), x[Tp:].reshape(Bs, Ls, D),
            jnp.stack(gla_p), jnp.stack(gla_s), jnp.stack(gmlp_s),
            ckv_p, kpe_p, ckv_s, kpe_s)
```

```python
import functools

import jax
import jax.numpy as jnp
from jax import lax
from jax.experimental import pallas as pl
from jax.experimental.pallas import tpu as pltpu

F32 = jnp.float32
BF16 = jnp.bfloat16

GLA_HEADS = 4
GLA_DK = 128
GLA_DV = 256
GLA_GATE_RANK = 16
GLA_TAU = 16.0
GLA_CHUNK = 64
GMLP_GROUPS = 4
GMLP_DG = 256
GMLP_CHUNK = 128
MLA_HEADS = 16
Q_LORA = 512
KV_LORA = 512
QK_NOPE = 128
QK_ROPE = 64
V_HEAD = 128
ROPE_THETA = 10000.0
MLA_SCALE = (QK_NOPE + QK_ROPE) ** -0.5
PEER_HEADS = 8
PEER_NKEYS = 128
PEER_TOPK = 16
LN_EPS = 1e-5
RMS_EPS = 1e-6
NEG_INF = -1e30

LANES = 128
SUBLANES = 8
VMEM_LIMIT_BYTES = 56 * 1024 * 1024

GLOW_PAD = 256


def _cparams(sem):
    return pltpu.CompilerParams(dimension_semantics=sem, vmem_limit_bytes=VMEM_LIMIT_BYTES)


def _pick(n, prefs):
    for p in prefs:
        if n % p == 0:
            return p
    raise ValueError(f"no tile in {prefs} divides {n}")


def _gelu(x):
    return 0.5 * x * (1.0 + jnp.tanh(0.7978845608028654 * (x + 0.044715 * (x * x * x))))


def _dot(a, b, **kw):
    return jnp.dot(a, b, preferred_element_type=F32, **kw)


def _dot_nt(a, b, **kw):
    return lax.dot_general(a, b, (((1,), (1,)), ((), ())), preferred_element_type=F32, **kw)


def _dot_tn(a, b, **kw):
    return lax.dot_general(a, b, (((0,), (0,)), ((), ())), preferred_element_type=F32, **kw)


def _mm_kernel(x_ref, w_ref, o_ref, *, scale):
    acc = _dot(x_ref[...], w_ref[...])
    if scale != 1.0:
        acc = acc * scale
    o_ref[...] = acc.astype(o_ref.dtype)


def mm(x, w, out_dtype, *, rows=None, row_start=0, scale=1.0, tn_prefs=(1792, 1024, 768, 512, 384, 256, 128)):
    M, K = x.shape
    N = w.shape[1]
    rows = M - row_start if rows is None else rows
    tm = _pick(rows, (1024, 512, 256, 128))
    assert row_start % tm == 0
    tn = _pick(N, tn_prefs)
    r0 = row_start // tm
    return pl.pallas_call(
        functools.partial(_mm_kernel, scale=scale),
        grid=(rows // tm, N // tn),
        in_specs=[pl.BlockSpec((tm, K), lambda i, j: (i + r0, 0)),
                  pl.BlockSpec((K, tn), lambda i, j: (0, j))],
        out_specs=pl.BlockSpec((tm, tn), lambda i, j: (i, j)),
        out_shape=jax.ShapeDtypeStruct((rows, N), out_dtype),
        compiler_params=_cparams(("parallel", "parallel")),
    )(x, w)


def _mm2_kernel(x1_ref, w1_ref, x2_ref, w2_ref, o_ref):
    o_ref[...] = _dot(x1_ref[...], w1_ref[...]) + _dot(x2_ref[...], w2_ref[...])


def mm2(x1, w1, x2, w2):
    M, K1 = x1.shape
    K2 = x2.shape[1]
    N = w1.shape[1]
    tm = _pick(M, (1024, 512, 256, 128))
    tn = _pick(N, (1024, 512, 256, 128))
    return pl.pallas_call(
        _mm2_kernel,
        grid=(M // tm, N // tn),
        in_specs=[pl.BlockSpec((tm, K1), lambda i, j: (i, 0)),
                  pl.BlockSpec((K1, tn), lambda i, j: (0, j)),
                  pl.BlockSpec((tm, K2), lambda i, j: (i, 0)),
                  pl.BlockSpec((K2, tn), lambda i, j: (0, j))],
        out_specs=pl.BlockSpec((tm, tn), lambda i, j: (i, j)),
        out_shape=jax.ShapeDtypeStruct((M, N), F32),
        compiler_params=_cparams(("parallel", "parallel")),
    )(x1, w1, x2, w2)


def _headmm_kernel(x_ref, w_ref, o_ref, *, transpose_w):
    if transpose_w:
        o_ref[...] = _dot_nt(x_ref[...], w_ref[...]).astype(o_ref.dtype)
    else:
        o_ref[...] = _dot(x_ref[...], w_ref[...]).astype(o_ref.dtype)


def head_mm(x, w, heads, kin, kout, *, transpose_w, row_start=0, rows=None):
    M = x.shape[0]
    rows = M - row_start if rows is None else rows
    tm = _pick(rows, (1024, 512, 256, 128))
    assert row_start % tm == 0
    r0 = row_start // tm
    wblock = (kout, kin) if transpose_w else (kin, kout)
    return pl.pallas_call(
        functools.partial(_headmm_kernel, transpose_w=transpose_w),
        grid=(rows // tm, heads),
        in_specs=[pl.BlockSpec((tm, kin), lambda i, h: (i + r0, h)),
                  pl.BlockSpec(wblock, lambda i, h: (0, h))],
        out_specs=pl.BlockSpec((tm, kout), lambda i, h: (i, h)),
        out_shape=jax.ShapeDtypeStruct((rows, heads * kout), BF16),
        compiler_params=_cparams(("parallel", "parallel")),
    )(x, w)


def _ln_kernel(x_ref, y_ref, g_ref, b_ref, of_ref, ob_ref, *, alpha):
    z = alpha * x_ref[...] + y_ref[...]
    mu = jnp.mean(z, axis=-1, keepdims=True)
    zc = z - mu
    var = jnp.mean(zc * zc, axis=-1, keepdims=True)
    out = zc * lax.rsqrt(var + LN_EPS) * g_ref[...] + b_ref[...]
    of_ref[...] = out
    ob_ref[...] = out.astype(BF16)


def residual_ln(x, y, g, b, alpha):
    T, D = x.shape
    tm = _pick(T, (256, 128))
    row = pl.BlockSpec((tm, D), lambda i: (i, 0))
    vec = pl.BlockSpec((1, D), lambda i: (0, 0))
    return pl.pallas_call(
        functools.partial(_ln_kernel, alpha=alpha),
        grid=(T // tm,),
        in_specs=[row, row, vec, vec],
        out_specs=[row, row],
        out_shape=[jax.ShapeDtypeStruct((T, D), F32), jax.ShapeDtypeStruct((T, D), BF16)],
        compiler_params=_cparams(("parallel",)),
    )(x, y, g.reshape(1, D), b.reshape(1, D))


def _gla_kernel(q_ref, k_ref, v_ref, gl_ref, r_ref, wg_ref, bg_ref, gn_ref, s0_ref,
                o_ref, s_ref, state, *, chunk, n_chunks, mxu_dtype):
    rb = pl.program_id(1)

    @pl.when(rb == 0)
    def _():
        state[...] = s0_ref[0]

    C = chunk
    row = lax.broadcasted_iota(jnp.int32, (C, C), 0)
    col = lax.broadcasted_iota(jnp.int32, (C, C), 1)
    causal = row >= col
    tri = causal.astype(F32)
    ones_cv = jnp.ones((C, GLA_DV), F32)
    hi = lax.Precision.HIGHEST
    md = mxu_dtype

    for c in range(n_chunks):
        sl = slice(c * C, (c + 1) * C)
        glow = gl_ref[sl, :].astype(BF16)
        outs, states = [], []
        for h in range(GLA_HEADS):
            kcols = slice(h * GLA_DK, (h + 1) * GLA_DK)
            vcols = slice(h * GLA_DV, (h + 1) * GLA_DV)
            qi = q_ref[sl, kcols] * (GLA_DK ** -0.5)
            ki = k_ref[sl, kcols]
            vi = v_ref[sl, vcols]
            z = _dot(glow, wg_ref[:, kcols]) + bg_ref[:, kcols]
            gi = (jnp.minimum(z, 0.0) - jnp.log(1.0 + jnp.exp(-jnp.abs(z)))) * (1.0 / GLA_TAU)
            b = _dot(tri, gi, precision=hi)
            b_last = b[C - 1:C, :]
            q_g = qi * jnp.exp(b)
            k_g = ki * jnp.exp(-b)
            att = jnp.where(causal, _dot_nt(q_g.astype(md), k_g.astype(md)), 0.0)
            s_prev = state[h]
            o = _dot(att.astype(md), vi.astype(md)) + _dot(q_g.astype(md), s_prev.astype(md))
            k_dec = ki * jnp.exp(b_last - b)
            decay = _dot_tn(gi, ones_cv, precision=hi)
            states.append(jnp.exp(decay) * s_prev + _dot_tn(k_dec.astype(md), vi.astype(md)))
            ms = jnp.mean(o * o, axis=-1, keepdims=True)
            on = o * lax.rsqrt(ms + RMS_EPS) * gn_ref[...]
            rr = r_ref[sl, vcols]
            outs.append((on * (rr * jax.nn.sigmoid(rr))).astype(o_ref.dtype))
        for h in range(GLA_HEADS):
            state[h] = states[h]
            o_ref[sl, h * GLA_DV:(h + 1) * GLA_DV] = outs[h]

    @pl.when(rb == pl.num_programs(1) - 1)
    def _():
        s_ref[0] = state[...]


def gla(hm, wg, bg, gn, s0, *, row_start, batch, seq):
    C = min(GLA_CHUNK, seq)
    assert seq % C == 0
    n_chunks = _pick(seq // C, (4, 2, 1))
    R = C * n_chunks
    nrb = seq // R
    assert row_start % R == 0
    r0 = row_start // R
    H = GLA_HEADS
    QW, VW = H * GLA_DK, H * GLA_DV
    glow_blk = (2 * QW + 2 * VW + 2 * GMLP_GROUPS * GMLP_DG) // LANES
    rowmap = lambda blk: (lambda b, rb: (r0 + b * nrb + rb, blk))
    mxu_dtype = BF16 if C >= 16 else F32
    out_dtype = BF16 if R >= 16 else F32
    kernel = functools.partial(_gla_kernel, chunk=C, n_chunks=n_chunks, mxu_dtype=mxu_dtype)
    return pl.pallas_call(
        kernel,
        grid=(batch, nrb),
        in_specs=[
            pl.BlockSpec((R, QW), rowmap(0)),
            pl.BlockSpec((R, QW), rowmap(1)),
            pl.BlockSpec((R, VW), rowmap(2 * QW // VW)),
            pl.BlockSpec((R, LANES), rowmap(glow_blk)),
            pl.BlockSpec((R, VW), rowmap(2 * QW // VW + 1)),
            pl.BlockSpec((LANES, QW), lambda b, rb: (0, 0)),
            pl.BlockSpec((1, QW), lambda b, rb: (0, 0)),
            pl.BlockSpec((1, GLA_DV), lambda b, rb: (0, 0)),
            pl.BlockSpec((1, H, GLA_DK, GLA_DV), lambda b, rb: (b, 0, 0, 0)),
        ],
        out_specs=[
            pl.BlockSpec((R, VW), lambda b, rb: (b * nrb + rb, 0)),
            pl.BlockSpec((1, H, GLA_DK, GLA_DV), lambda b, rb: (b, 0, 0, 0)),
        ],
        out_shape=[jax.ShapeDtypeStruct((batch * seq, VW), out_dtype),
                   jax.ShapeDtypeStruct((batch, H, GLA_DK, GLA_DV), F32)],
        scratch_shapes=[pltpu.VMEM((H, GLA_DK, GLA_DV), F32)],
        compiler_params=_cparams(("parallel", "arbitrary")),
    )(hm, hm, hm, hm, hm, wg, bg, gn, s0)


def _gmlp_kernel(gu_ref, gv_ref, lng_ref, lnb_ref, w_ref, b_ref, m_ref, *gvo_ref):
    u = _gelu(gu_ref[...])
    v = _gelu(gv_ref[...])
    mu = jnp.mean(v, axis=-1, keepdims=True)
    vc = v - mu
    var = jnp.mean(vc * vc, axis=-1, keepdims=True)
    vn = vc * lax.rsqrt(var + LN_EPS) * lng_ref[0] + lnb_ref[0]
    s = _dot(w_ref[0], vn.astype(BF16)) + b_ref[0]
    m_ref[...] = (u * s).astype(m_ref.dtype)
    if gvo_ref:
        gvo_ref[0][...] = vn


def gmlp(hm, lng, lnb, w_eff, b_eff, *, row_start, rows, emit_v):
    G = GMLP_GROUPS
    R = GMLP_CHUNK
    assert rows % R == 0 and row_start % R == 0
    r0 = row_start // R
    ublk = (2 * GLA_HEADS * GLA_DK + 2 * GLA_HEADS * GLA_DV) // GMLP_DG
    vblk = ublk + G
    out_block = pl.BlockSpec((R, GMLP_DG), lambda i, g: (i, g))
    out_specs = [out_block]
    out_shape = [jax.ShapeDtypeStruct((rows, G * GMLP_DG), BF16)]
    if emit_v:
        out_specs.append(out_block)
        out_shape.append(jax.ShapeDtypeStruct((rows, G * GMLP_DG), F32))
    return pl.pallas_call(
        _gmlp_kernel,
        grid=(rows // R, G),
        in_specs=[
            pl.BlockSpec((R, GMLP_DG), lambda i, g: (r0 + i, ublk + g)),
            pl.BlockSpec((R, GMLP_DG), lambda i, g: (r0 + i, vblk + g)),
            pl.BlockSpec((1, 1, GMLP_DG), lambda i, g: (g, 0, 0)),
            pl.BlockSpec((1, 1, GMLP_DG), lambda i, g: (g, 0, 0)),
            pl.BlockSpec((1, R, R), lambda i, g: (g, 0, 0)),
            pl.BlockSpec((1, R, 1), lambda i, g: (g, 0, 0)),
        ],
        out_specs=out_specs,
        out_shape=out_shape,
        compiler_params=_cparams(("parallel", "parallel")),
    )(hm, hm, lng.reshape(G, 1, GMLP_DG), lnb.reshape(G, 1, GMLP_DG), w_eff, b_eff.reshape(G, R, 1))


def _mla_post_kernel(h_ref, cs_ref, qg_ref, kg_ref, cq_ref, ckv_ref, ckvb_ref, kpe_ref, kpeb_ref):
    cq = h_ref[:, :Q_LORA]
    ckv = h_ref[:, Q_LORA:Q_LORA + KV_LORA]
    cqn = cq * lax.rsqrt(jnp.mean(cq * cq, axis=-1, keepdims=True) + RMS_EPS) * qg_ref[...]
    ckvn = ckv * lax.rsqrt(jnp.mean(ckv * ckv, axis=-1, keepdims=True) + RMS_EPS) * kg_ref[...]
    cq_ref[...] = cqn.astype(BF16)
    ckv_ref[...] = ckvn
    ckvb_ref[...] = ckvn.astype(BF16)
    t = h_ref[:, Q_LORA + KV_LORA:] * cs_ref[...]
    rot = t + pltpu.roll(t, QK_ROPE, 1)
    lane = lax.broadcasted_iota(jnp.int32, rot.shape, 1)
    rot = jnp.where(lane < QK_ROPE, rot, 0.0)
    kpe_ref[...] = rot
    kpeb_ref[...] = rot.astype(BF16)


def mla_post(h, cs, q_norm_g, kv_norm_g):
    T = h.shape[0]
    tm = _pick(T, (512, 256, 128))
    W = h.shape[1]
    return pl.pallas_call(
        _mla_post_kernel,
        grid=(T // tm,),
        in_specs=[pl.BlockSpec((tm, W), lambda i: (i, 0)),
                  pl.BlockSpec((tm, LANES), lambda i: (i, 0)),
                  pl.BlockSpec((1, Q_LORA), lambda i: (0, 0)),
                  pl.BlockSpec((1, KV_LORA), lambda i: (0, 0))],
        out_specs=[pl.BlockSpec((tm, Q_LORA), lambda i: (i, 0)),
                   pl.BlockSpec((tm, KV_LORA), lambda i: (i, 0)),
                   pl.BlockSpec((tm, KV_LORA), lambda i: (i, 0)),
                   pl.BlockSpec((tm, LANES), lambda i: (i, 0)),
                   pl.BlockSpec((tm, LANES), lambda i: (i, 0))],
        out_shape=[jax.ShapeDtypeStruct((T, Q_LORA), BF16),
                   jax.ShapeDtypeStruct((T, KV_LORA), F32),
                   jax.ShapeDtypeStruct((T, KV_LORA), BF16),
                   jax.ShapeDtypeStruct((T, LANES), F32),
                   jax.ShapeDtypeStruct((T, LANES), BF16)],
        compiler_params=_cparams(("parallel",)),
    )(h, cs, q_norm_g.reshape(1, Q_LORA), kv_norm_g.reshape(1, KV_LORA))


def _qpe_kernel(cq_ref, w_ref, wsw_ref, cs_ref, o_ref):
    a = _dot(cq_ref[...], w_ref[...])
    b = _dot(cq_ref[...], wsw_ref[...])
    cos = cs_ref[:, :LANES]
    sin = cs_ref[:, LANES:]
    for h in range(MLA_HEADS):
        sl = slice(h * LANES, (h + 1) * LANES)
        o_ref[:, sl] = ((a[:, sl] * cos + b[:, sl] * sin) * MLA_SCALE).astype(o_ref.dtype)


def q_rope(cq, w_pe, w_pe_sw, cs2):
    T, K = cq.shape
    N = w_pe.shape[1]
    tm = _pick(T, (512, 256, 128))
    return pl.pallas_call(
        _qpe_kernel,
        grid=(T // tm,),
        in_specs=[pl.BlockSpec((tm, K), lambda i: (i, 0)),
                  pl.BlockSpec((K, N), lambda i: (0, 0)),
                  pl.BlockSpec((K, N), lambda i: (0, 0)),
                  pl.BlockSpec((tm, 2 * LANES), lambda i: (i, 0))],
        out_specs=pl.BlockSpec((tm, N), lambda i: (i, 0)),
        out_shape=jax.ShapeDtypeStruct((T, N), BF16),
        compiler_params=_cparams(("parallel",)),
    )(cq, w_pe, w_pe_sw, cs2)


def _flash_kernel(qn_ref, qp_ref, kn_ref, kp_ref, vt_ref, o_ref, m_scr, l_scr, acc_scr, *, tq):
    qi = pl.program_id(2)
    q = jnp.concatenate([qn_ref[...], qp_ref[...]], axis=1)
    m_scr[...] = jnp.full_like(m_scr, NEG_INF)
    l_scr[...] = jnp.zeros_like(l_scr)
    acc_scr[...] = jnp.zeros_like(acc_scr)

    def block(ki, n_blocks, masked):
        start = pl.multiple_of(ki * tq, tq)
        width = n_blocks * tq
        k = jnp.concatenate([kn_ref[pl.ds(start, width), :], kp_ref[pl.ds(start, width), :]], axis=1)
        st = _dot_nt(k, q)
        if masked:
            key = lax.broadcasted_iota(jnp.int32, st.shape, 0)
            qry = lax.broadcasted_iota(jnp.int32, st.shape, 1)
            st = jnp.where(key <= qry, st, NEG_INF)
        m_prev = m_scr[...]
        m_new = jnp.maximum(m_prev, jnp.max(st, axis=0, keepdims=True))
        alpha = jnp.exp(m_prev - m_new)
        pf = jnp.exp(st - m_new)
        p = pf.astype(BF16)
        pv = _dot(vt_ref[ki], p[:tq])
        for i in range(1, n_blocks):
            pv = pv + _dot(vt_ref[ki + i], p[i * tq:(i + 1) * tq])
        l_scr[...] = alpha * l_scr[...] + jnp.sum(pf, axis=0, keepdims=True)
        acc_scr[...] = alpha * acc_scr[...] + pv
        m_scr[...] = m_new

    def body(kp, carry):
        block(2 * kp, 2, False)
        return carry

    lax.fori_loop(0, qi // 2, body, 0)

    @pl.when(qi % 2 == 1)
    def _():
        block(qi - 1, 1, False)

    block(qi, 1, True)
    o_ref[...] = (acc_scr[...] / l_scr[...]).T.astype(o_ref.dtype)


def _vt_kernel(w_ref, c_ref, o_ref):
    o_ref[0] = _dot_nt(w_ref[...], c_ref[...]).astype(o_ref.dtype)


def value_blocks_t(ckv, w_uv_t, *, rows, tk):
    K = ckv.shape[1]
    N = w_uv_t.shape[0]
    return pl.pallas_call(
        _vt_kernel,
        grid=(rows // tk,),
        in_specs=[pl.BlockSpec((N, K), lambda i: (0, 0)),
                  pl.BlockSpec((tk, K), lambda i: (i, 0))],
        out_specs=pl.BlockSpec((1, N, tk), lambda i: (i, 0, 0)),
        out_shape=jax.ShapeDtypeStruct((rows // tk, N, tk), BF16),
        compiler_params=_cparams(("parallel",)),
    )(w_uv_t, ckv)


def mla_prompt_attention(qn, qp, kn, kp, vt, *, batch, seq, tq):
    nq = seq // tq
    H = MLA_HEADS
    return pl.pallas_call(
        functools.partial(_flash_kernel, tq=tq),
        grid=(batch, H, nq),
        in_specs=[pl.BlockSpec((tq, LANES), lambda b, h, qi: (b * nq + qi, h)),
                  pl.BlockSpec((tq, LANES), lambda b, h, qi: (b * nq + qi, h)),
                  pl.BlockSpec((seq, LANES), lambda b, h, qi: (b, h)),
                  pl.BlockSpec((seq, LANES), lambda b, h, qi: (b, 0)),
                  pl.BlockSpec((nq, V_HEAD, tq), lambda b, h, qi: (b, h, 0))],
        out_specs=pl.BlockSpec((tq, LANES), lambda b, h, qi: (b * nq + qi, h)),
        out_shape=jax.ShapeDtypeStruct((batch * seq, H * V_HEAD), BF16),
        scratch_shapes=[pltpu.VMEM((1, tq), F32), pltpu.VMEM((1, tq), F32), pltpu.VMEM((V_HEAD, tq), F32)],
        compiler_params=_cparams(("parallel", "parallel", "arbitrary")),
    )(qn, qp, kn, kp, vt)


def _paged_kernel(pt_ref, ql_ref, qp_ref, cn_ref, kn_ref, *rest, pages_per_step, dec_seq):
    P = pages_per_step
    ckv_refs = rest[:P]
    kpe_refs = rest[P:2 * P]
    o_ref = rest[2 * P]
    m_scr, l_scr, acc_scr = rest[2 * P + 1:]
    j = pl.program_id(1)
    rows = dec_seq * MLA_HEADS

    @pl.when(j == 0)
    def _():
        m_scr[...] = jnp.full_like(m_scr, NEG_INF)
        l_scr[...] = jnp.zeros_like(l_scr)
        acc_scr[...] = jnp.zeros_like(acc_scr)

    ql = ql_ref[...].reshape(rows, KV_LORA)
    qp = qp_ref[...].reshape(rows, LANES)
    qp_rope = qp[:, :QK_ROPE]

    kc = jnp.concatenate([r[0].astype(BF16) for r in ckv_refs], axis=0)
    kk_t = jnp.concatenate([r[0].astype(BF16) for r in kpe_refs], axis=1)
    s = _dot_nt(ql, kc) + _dot(qp_rope, kk_t)
    m_prev = m_scr[...]
    m_new = jnp.maximum(m_prev, jnp.max(s, axis=1, keepdims=True))
    alpha = jnp.exp(m_prev - m_new)
    p = jnp.exp(s - m_new)
    l_scr[...] = alpha * l_scr[...] + jnp.sum(p, axis=1, keepdims=True)
    acc_scr[...] = alpha * acc_scr[...] + _dot(p.astype(BF16), kc)
    m_scr[...] = m_new

    @pl.when(j == pl.num_programs(1) - 1)
    def _():
        cn = cn_ref[...]
        kn = kn_ref[...]
        sn = _dot_nt(ql.astype(F32), cn) + _dot_nt(qp.astype(F32), kn)
        t_row = lax.broadcasted_iota(jnp.int32, sn.shape, 0) // MLA_HEADS
        t_col = lax.broadcasted_iota(jnp.int32, sn.shape, 1)
        sn = jnp.where(t_col <= t_row, sn, NEG_INF)
        m_prev = m_scr[...]
        m_new = jnp.maximum(m_prev, jnp.max(sn, axis=1, keepdims=True))
        alpha = jnp.exp(m_prev - m_new)
        pn = jnp.exp(sn - m_new)
        l = alpha * l_scr[...] + jnp.sum(pn, axis=1, keepdims=True)
        acc = alpha * acc_scr[...] + _dot(pn, cn)
        o_ref[...] = (acc / l).reshape(dec_seq, MLA_HEADS, KV_LORA).astype(o_ref.dtype)


def mla_sample_attention(page_table, q_lat, qp3, ckv_f32, kpe_f32, cache_ckv, cache_kpe, page_base, *, row_start, dec_seq):
    B, n_pages = page_table.shape
    page = cache_ckv.shape[1]
    P = _pick(n_pages, (16, 8, 4, 2, 1))
    nsteps = n_pages // P
    assert row_start % dec_seq == 0
    r0 = row_start // dec_seq
    H = MLA_HEADS

    def page_map(p):
        return lambda b, j, pt: (page_base + pt[b * n_pages + j * P + p], 0, 0)

    in_specs = [
        pl.BlockSpec((dec_seq, H, KV_LORA), lambda b, j, pt: (b, 0, 0)),
        pl.BlockSpec((dec_seq, H, LANES), lambda b, j, pt: (r0 + b, 0, 0)),
        pl.BlockSpec((dec_seq, KV_LORA), lambda b, j, pt: (r0 + b, 0)),
        pl.BlockSpec((dec_seq, LANES), lambda b, j, pt: (r0 + b, 0)),
    ]
    in_specs += [pl.BlockSpec((1, page, KV_LORA), page_map(p)) for p in range(P)]
    in_specs += [pl.BlockSpec((1, QK_ROPE, page), page_map(p)) for p in range(P)]
    rows = dec_seq * H
    grid_spec = pltpu.PrefetchScalarGridSpec(
        num_scalar_prefetch=1,
        grid=(B, nsteps),
        in_specs=in_specs,
        out_specs=pl.BlockSpec((dec_seq, H, KV_LORA), lambda b, j, pt: (b, 0, 0)),
        scratch_shapes=[pltpu.VMEM((rows, 1), F32), pltpu.VMEM((rows, 1), F32), pltpu.VMEM((rows, KV_LORA), F32)],
    )
    return pl.pallas_call(
        functools.partial(_paged_kernel, pages_per_step=P, dec_seq=dec_seq),
        grid_spec=grid_spec,
        out_shape=jax.ShapeDtypeStruct((B * dec_seq, H, KV_LORA), BF16),
        compiler_params=_cparams(("parallel", "arbitrary")),
    )(page_table.reshape(-1), q_lat, qp3, ckv_f32, kpe_f32, *([cache_ckv] * P), *([cache_kpe] * P))


RANK_BASE = 1e30
RANK_STEP = 1e28


def _top_values(s, k, with_rank=False):
    out = []
    for r in range(k):
        m = jnp.max(s, axis=0, keepdims=True)
        out.append(m)
        s = jnp.where(s == m, -(RANK_BASE + r * RANK_STEP) if with_rank else -jnp.inf, s)
    if not with_rank:
        return out
    rank = jnp.where(s <= -RANK_BASE, jnp.round((-s - RANK_BASE) * (1.0 / RANK_STEP)), float(k))
    return out, rank


def _peer_topk_kernel(q_ref, keys_ref, r1_ref, c1_ref, r2_ref, e2_ref, cnt_ref):
    K = PEER_TOPK
    for h in range(PEER_HEADS):
        qa = q_ref[:, (2 * h) * LANES:(2 * h + 1) * LANES]
        qb = q_ref[:, (2 * h + 1) * LANES:(2 * h + 2) * LANES]
        sa = _dot_nt(keys_ref[2 * h], qa)
        sb = _dot_nt(keys_ref[2 * h + 1], qb)
        ta, rank_a = _top_values(sa, K, with_rank=True)
        tb, rank_b = _top_values(sb, K, with_rank=True)
        ta_all = jnp.concatenate(ta, axis=0)
        tb_all = jnp.concatenate(tb, axis=0)
        half = K // 2
        rows_a0 = ta[0] + tb_all
        rows_lo = [ta[a] + tb_all[:half] for a in range(1, half)]
        rows_hi = ta_all[half:] + tb[0]
        best = _top_values(jnp.concatenate([rows_a0] + rows_lo + [rows_hi], axis=0), K)
        tau = best[K - 1]
        z = jnp.zeros_like(best[0])
        for r in range(K):
            z = z + jnp.exp(best[r] - best[0])
        cnt = [jnp.sum((rows >= tau).astype(F32), axis=0, keepdims=True) for rows in [rows_a0] + rows_lo]
        cnt.append((rows_hi >= tau).astype(F32))
        r1_ref[h] = rank_a
        r2_ref[h] = rank_b.astype(r2_ref.dtype)
        c1_ref[h] = jnp.exp(sa - ta[0]) / z
        e2_ref[h] = jnp.exp(sb - tb[0]).astype(e2_ref.dtype)
        cnt_ref[h] = jnp.concatenate(cnt, axis=0)


def peer_topk(q, keys):
    T = q.shape[0]
    TT = _pick(T, (512, 256, 128))
    blk = pl.BlockSpec((PEER_HEADS, PEER_NKEYS, TT), lambda i: (0, 0, i))
    f32 = jax.ShapeDtypeStruct((PEER_HEADS, PEER_NKEYS, T), F32)
    b16 = jax.ShapeDtypeStruct((PEER_HEADS, PEER_NKEYS, T), BF16)
    return pl.pallas_call(
        _peer_topk_kernel,
        grid=(T // TT,),
        in_specs=[pl.BlockSpec((TT, q.shape[1]), lambda i: (i, 0)),
                  pl.BlockSpec(keys.shape, lambda i: (0, 0, 0))],
        out_specs=[blk, blk, blk, blk, pl.BlockSpec((PEER_HEADS, PEER_TOPK, TT), lambda i: (0, 0, i))],
        out_shape=[f32, f32, b16, b16, jax.ShapeDtypeStruct((PEER_HEADS, PEER_TOPK, T), F32)],
        compiler_params=_cparams(("parallel",)),
    )(q, keys)


PEER_TE = 8 * PEER_NKEYS


def _peer_dense_kernel(x_ref, u_ref, vt_ref, r1_ref, c1_ref, r2_ref, e2_ref, cnt_ref, o_ref, acc_scr, a_scr):
    j = pl.program_id(1)

    @pl.when(j == 0)
    def _():
        acc_scr[...] = jnp.zeros_like(acc_scr)

    x = x_ref[...]
    TT = x.shape[0]
    rank_id = lax.broadcasted_iota(jnp.int32, (PEER_TOPK, TT), 0).astype(F32)
    zero = jnp.zeros((), BF16)
    gps = 2
    for g in range(PEER_TE // PEER_NKEYS):
        if g % gps == 0:
            slab = slice(g * PEER_NKEYS, (g + gps) * PEER_NKEYS)
            ht = _dot_nt(u_ref[0, slab, :], x)
        gate = None
        for h in range(PEER_HEADS):
            r1 = r1_ref[h, g:g + 1, :]
            k1 = jnp.sum(jnp.where(r1 == rank_id, cnt_ref[h], 0.0), axis=0, keepdims=True)
            c1 = c1_ref[h, g:g + 1, :].astype(BF16)
            term = jnp.where(r2_ref[h] < k1.astype(BF16), e2_ref[h] * c1, zero)
            gate = term if gate is None else gate + term
        hs = ht[(g % gps) * PEER_NKEYS:(g % gps + 1) * PEER_NKEYS, :]
        a_scr[g * PEER_NKEYS:(g + 1) * PEER_NKEYS, :] = _gelu(hs).astype(BF16) * gate
    acc_scr[...] += _dot(vt_ref[0], a_scr[...])

    @pl.when(j == pl.num_programs(1) - 1)
    def _():
        o_ref[...] = acc_scr[...].T


def peer_dense(x, u, vt, layer, r1, c1, r2, e2, cnt):
    T, D = x.shape
    E = u.shape[1]
    TT = _pick(T, (512, 256, 128))
    TE = PEER_TE
    assert E % TE == 0 and E == PEER_NKEYS * PEER_NKEYS
    full = pl.BlockSpec((PEER_HEADS, PEER_NKEYS, TT), lambda i, j: (0, 0, i))
    rows8 = pl.BlockSpec((PEER_HEADS, SUBLANES, TT), lambda i, j: (0, j, i))
    return pl.pallas_call(
        _peer_dense_kernel,
        grid=(T // TT, E // TE),
        in_specs=[pl.BlockSpec((TT, D), lambda i, j: (i, 0)),
                  pl.BlockSpec((1, TE, D), lambda i, j: (layer, j, 0)),
                  pl.BlockSpec((1, D, TE), lambda i, j: (layer, 0, j)),
                  rows8, rows8, full, full,
                  pl.BlockSpec((PEER_HEADS, PEER_TOPK, TT), lambda i, j: (0, 0, i))],
        out_specs=pl.BlockSpec((TT, D), lambda i, j: (i, 0)),
        out_shape=jax.ShapeDtypeStruct((T, D), F32),
        scratch_shapes=[pltpu.VMEM((D, TT), F32), pltpu.VMEM((TE, TT), BF16)],
        compiler_params=_cparams(("parallel", "arbitrary")),
    )(x, u, vt, r1, c1, r2, e2, cnt)


def _cast_transpose_kernel(v_ref, o_ref):
    o_ref[0] = v_ref[0].T.astype(o_ref.dtype)


def cast_transpose(v):
    L, E, D = v.shape
    te = _pick(E, (512, 256, 128))
    return pl.pallas_call(
        _cast_transpose_kernel,
        grid=(L, E // te),
        in_specs=[pl.BlockSpec((1, te, D), lambda l, j: (l, j, 0))],
        out_specs=pl.BlockSpec((1, D, te), lambda l, j: (l, 0, j)),
        out_shape=jax.ShapeDtypeStruct((L, D, E), BF16),
        compiler_params=_cparams(("parallel", "parallel")),
    )(v)


def _rope_swap_cols(w):
    half = QK_ROPE // 2
    return jnp.concatenate([-w[..., half:], w[..., :half]], axis=-1)


def _prep_even(w_in, w_gate_up, w_out):
    qk = 2 * GLA_HEADS * GLA_DK
    vw = GLA_HEADS * GLA_DV
    g0 = qk + vw
    g1 = g0 + GLA_GATE_RANK
    K = w_in.shape[0]
    w_perm = jnp.concatenate(
        [w_in[:, :g0], w_in[:, g1:], w_in[:, g0:g1], jnp.zeros((K, GLOW_PAD - GLA_GATE_RANK), w_in.dtype)], axis=1)
    wg = jnp.concatenate([w_gate_up, jnp.zeros((LANES - GLA_GATE_RANK, w_gate_up.shape[1]), w_gate_up.dtype)], axis=0)
    return w_perm.astype(BF16), wg.astype(BF16), w_out[:vw].astype(BF16), w_out[vw:].astype(BF16)


def _prep_gmlp_spatial(w_sp, b_sp, seq):
    C = min(GMLP_CHUNK, seq)
    assert seq % C == 0 and GMLP_CHUNK % C == 0
    w = jnp.where(jnp.tril(jnp.ones((C, C), bool)), w_sp[:, :C, :C], 0.0)
    b = b_sp[:, :C]
    rep = GMLP_CHUNK // C
    if rep > 1:
        eye = jnp.eye(rep, dtype=w.dtype)
        w = jnp.einsum('ab,gij->gaibj', eye, w).reshape(w.shape[0], GMLP_CHUNK, GMLP_CHUNK)
        b = jnp.tile(b, (1, rep))
    return w.astype(BF16), b


def _prep_odd(w_in, w_uq, w_uk, w_uv, w_o):
    kpe_w = w_in[:, Q_LORA + KV_LORA:]
    w_in_p = jnp.concatenate([w_in, _rope_swap_cols(kpe_w)], axis=1).astype(BF16)
    w_n = w_uq[:, :, :QK_NOPE].reshape(Q_LORA, MLA_HEADS * QK_NOPE).astype(BF16)
    pe = w_uq[:, :, QK_NOPE:]
    zpad = jnp.zeros((Q_LORA, MLA_HEADS, LANES - QK_ROPE), w_uq.dtype)
    w_pe = jnp.concatenate([pe, zpad], axis=-1).reshape(Q_LORA, MLA_HEADS * LANES).astype(BF16)
    w_pe_sw = jnp.concatenate([_rope_swap_cols(pe), zpad], axis=-1).reshape(Q_LORA, MLA_HEADS * LANES).astype(BF16)
    w_uk2 = w_uk.reshape(KV_LORA, MLA_HEADS * QK_NOPE).astype(BF16)
    w_uv2 = w_uv.reshape(KV_LORA, MLA_HEADS * V_HEAD).astype(BF16)
    return w_in_p, w_n, w_pe, w_pe_sw, w_uk2, w_uv2, w_o.astype(BF16)


def _rope_tables(pos):
    inv = ROPE_THETA ** (-jnp.arange(0, QK_ROPE, 2, dtype=F32) / QK_ROPE)
    ang = pos.astype(F32)[:, None] * inv[None, :]
    cos, sin = jnp.cos(ang), jnp.sin(ang)
    z = jnp.zeros((pos.shape[0], LANES - QK_ROPE), F32)
    cs_k = jnp.concatenate([cos, cos, sin, sin], axis=1)
    cs_q = jnp.concatenate([cos, cos, z, sin, sin, z], axis=1)
    return cs_k, cs_q


def kernel(x_prompt, x_sample, state_gla, cache_ckv, cache_kpe, page_table, w_in_even, w_gate_up, b_gate, gla_norm_g, gmlp_ln_g, gmlp_ln_b, w_spatial, b_spatial, w_out_even, w_in_odd, q_norm_g, kv_norm_g, w_uq, w_uk, w_uv, w_o_odd, ln_mix_g, ln_mix_b, ln_ffn_g, ln_ffn_b, w_peer_q, peer_keys, peer_u, peer_v):
    B, L, D = x_prompt.shape
    Bs, Ls, _ = x_sample.shape
    Tp, Ts = B * L, Bs * Ls
    T = Tp + Ts
    depth = ln_mix_g.shape[0]
    n_pool, page = cache_ckv.shape[1], cache_ckv.shape[2]
    past_len = page_table.shape[1] * page
    dn_alpha = (2.0 * depth) ** 0.25

    x = jnp.concatenate([x_prompt.reshape(Tp, D), x_sample.reshape(Ts, D)], axis=0)
    xb = x.astype(BF16)

    pos = jnp.concatenate([jnp.tile(jnp.arange(L, dtype=jnp.int32), B),
                           jnp.tile(past_len + jnp.arange(Ls, dtype=jnp.int32), Bs)])
    cs_k, cs_q = _rope_tables(pos)

    peer_u_b = peer_u.astype(BF16)
    peer_vt_b = cast_transpose(peer_v)
    cache_ckv2 = cache_ckv.reshape(-1, page, KV_LORA)
    cache_kpe2 = jnp.swapaxes(cache_kpe, 2, 3).reshape(-1, QK_ROPE, page)
    zeros_state = jnp.zeros((B, GLA_HEADS, GLA_DK, GLA_DV), F32)

    gla_p, gla_s, gmlp_s, ckv_out, kpe_out = [], [], [], [], []
    for l in range(depth):
        i = l // 2
        if l % 2 == 0:
            w_in, wg, w_out_o, w_out_m = _prep_even(w_in_even[i], w_gate_up[i], w_out_even[i])
            hm = mm(xb, w_in, F32)
            bg = b_gate[i].reshape(1, -1)
            gn = gla_norm_g[i].reshape(1, -1)
            o_p, s_p = gla(hm, wg, bg, gn, zeros_state, row_start=0, batch=B, seq=L)
            o_s, s_s = gla(hm, wg, bg, gn, state_gla[i], row_start=Tp, batch=Bs, seq=Ls)
            wsp_p, bsp_p = _prep_gmlp_spatial(w_spatial[i], b_spatial[i], L)
            wsp_s, bsp_s = _prep_gmlp_spatial(w_spatial[i], b_spatial[i], Ls)
            (m_p,) = gmlp(hm, gmlp_ln_g[i], gmlp_ln_b[i], wsp_p, bsp_p, row_start=0, rows=Tp, emit_v=False)
            m_s, gv_s = gmlp(hm, gmlp_ln_g[i], gmlp_ln_b[i], wsp_s, bsp_s, row_start=Tp, rows=Ts, emit_v=True)
            y = mm2(jnp.concatenate([o_p, o_s.astype(BF16)], axis=0), w_out_o, jnp.concatenate([m_p, m_s], axis=0), w_out_m)
            gla_p.append(s_p)
            gla_s.append(s_s)
            gmlp_s.append(gv_s.reshape(Bs, Ls, GMLP_GROUPS, GMLP_DG))
        else:
            w_in, w_n, w_pe, w_pe_sw, w_uk2, w_uv2, w_o = _prep_odd(w_in_odd[i], w_uq[i], w_uk[i], w_uv[i], w_o_odd[i])
            h = mm(xb, w_in, F32)
            cq, ckv, ckv_b, kpe, kpe_b = mla_post(h, cs_k, q_norm_g[i], kv_norm_g[i])
            qn = mm(cq, w_n, BF16, scale=MLA_SCALE)
            qp = q_rope(cq, w_pe, w_pe_sw, cs_q)
            kn = mm(ckv_b, w_uk2, BF16, rows=Tp)
            tq = _pick(L, (512, 256, 128))
            vt = value_blocks_t(ckv_b, w_uv2.T, rows=Tp, tk=tq)
            o_p = mla_prompt_attention(qn, qp, kn, kpe_b, vt, batch=B, seq=L, tq=tq)
            q_lat = head_mm(qn, w_uk2, MLA_HEADS, QK_NOPE, KV_LORA, transpose_w=True, row_start=Tp, rows=Ts)
            o_lat = mla_sample_attention(page_table, q_lat.reshape(Ts, MLA_HEADS, KV_LORA),
                                         qp.reshape(T, MLA_HEADS, LANES), ckv, kpe, cache_ckv2, cache_kpe2,
                                         i * n_pool, row_start=Tp, dec_seq=Ls)
            o_s = head_mm(o_lat.reshape(Ts, MLA_HEADS * KV_LORA), w_uv2, MLA_HEADS, KV_LORA, V_HEAD, transpose_w=False)
            y = mm(jnp.concatenate([o_p, o_s], axis=0), w_o, F32)
            ckv_out.append(ckv)
            kpe_out.append(kpe[:, :QK_ROPE])
        x, xb = residual_ln(x, y, ln_mix_g[l], ln_mix_b[l], dn_alpha)
        q = mm(xb, w_peer_q[l].astype(BF16), BF16)
        keys = peer_keys[l].reshape(PEER_HEADS * 2, PEER_NKEYS, -1).astype(BF16)
        r1, c1, r2, e2, cnt = peer_topk(q, keys)
        y = peer_dense(xb, peer_u_b, peer_vt_b, l, r1, c1, r2, e2, cnt)
        x, xb = residual_ln(x, y, ln_ffn_g[l], ln_ffn_b[l], dn_alpha)

    def split(rows_list, width):
        p = jnp.stack([r[:Tp].reshape(B, L, width) for r in rows_list])
        s = jnp.stack([r[Tp:].reshape(Bs, Ls, width) for r in rows_list])
        return p, s

    ckv_p, ckv_s = split(ckv_out, KV_LORA)
    kpe_p, kpe_s = split(kpe_out, QK_ROPE)
    return (x[:Tp].reshape(B, L, D), x[Tp:].reshape(Bs, Ls, D),
            jnp.stack(gla_p), jnp.stack(gla_s), jnp.stack(gmlp_s),
            ckv_p, kpe_p, ckv_s, kpe_s)
```

```python
import functools

import jax
import jax.numpy as jnp
from jax import lax
from jax.experimental import pallas as pl
from jax.experimental.pallas import tpu as pltpu

F32 = jnp.float32
BF16 = jnp.bfloat16

GLA_HEADS = 4
GLA_DK = 128
GLA_DV = 256
GLA_GATE_RANK = 16
GLA_TAU = 16.0
GLA_CHUNK = 64
GMLP_GROUPS = 4
GMLP_DG = 256
GMLP_CHUNK = 128
MLA_HEADS = 16
Q_LORA = 512
KV_LORA = 512
QK_NOPE = 128
QK_ROPE = 64
V_HEAD = 128
ROPE_THETA = 10000.0
MLA_SCALE = (QK_NOPE + QK_ROPE) ** -0.5
PEER_HEADS = 8
PEER_NKEYS = 128
PEER_TOPK = 16
LN_EPS = 1e-5
RMS_EPS = 1e-6
NEG_INF = -1e30

LANES = 128
SUBLANES = 8
VMEM_LIMIT_BYTES = 56 * 1024 * 1024

GLOW_PAD = 256


def _cparams(sem):
    return pltpu.CompilerParams(dimension_semantics=sem, vmem_limit_bytes=VMEM_LIMIT_BYTES)


def _pick(n, prefs):
    for p in prefs:
        if n % p == 0:
            return p
    raise ValueError(f"no tile in {prefs} divides {n}")


def _gelu(x):
    return 0.5 * x * (1.0 + jnp.tanh(0.7978845608028654 * (x + 0.044715 * (x * x * x))))


def _gelu_sigmoid(x):
    c = 2.0 * 0.7978845608028654
    neg2y = x * (x * x * (-c * 0.044715) - c)
    return x / (1.0 + jnp.exp(neg2y))


def _dot(a, b, **kw):
    return jnp.dot(a, b, preferred_element_type=F32, **kw)


def _dot_nt(a, b, **kw):
    return lax.dot_general(a, b, (((1,), (1,)), ((), ())), preferred_element_type=F32, **kw)


def _dot_tn(a, b, **kw):
    return lax.dot_general(a, b, (((0,), (0,)), ((), ())), preferred_element_type=F32, **kw)


def _mm_kernel(x_ref, w_ref, o_ref, *, scale):
    acc = _dot(x_ref[...], w_ref[...])
    if scale != 1.0:
        acc = acc * scale
    o_ref[...] = acc.astype(o_ref.dtype)


def mm(x, w, out_dtype, *, rows=None, row_start=0, scale=1.0, tn_prefs=(1792, 1024, 768, 512, 384, 256, 128)):
    M, K = x.shape
    N = w.shape[1]
    rows = M - row_start if rows is None else rows
    tm = _pick(rows, (1024, 512, 256, 128))
    assert row_start % tm == 0
    tn = _pick(N, tn_prefs)
    r0 = row_start // tm
    return pl.pallas_call(
        functools.partial(_mm_kernel, scale=scale),
        grid=(rows // tm, N // tn),
        in_specs=[pl.BlockSpec((tm, K), lambda i, j: (i + r0, 0)),
                  pl.BlockSpec((K, tn), lambda i, j: (0, j))],
        out_specs=pl.BlockSpec((tm, tn), lambda i, j: (i, j)),
        out_shape=jax.ShapeDtypeStruct((rows, N), out_dtype),
        compiler_params=_cparams(("parallel", "parallel")),
    )(x, w)


def _mm2_kernel(x1_ref, w1_ref, x2_ref, w2_ref, o_ref):
    o_ref[...] = _dot(x1_ref[...], w1_ref[...]) + _dot(x2_ref[...], w2_ref[...])


def mm2(x1, w1, x2, w2):
    M, K1 = x1.shape
    K2 = x2.shape[1]
    N = w1.shape[1]
    tm = _pick(M, (1024, 512, 256, 128))
    tn = _pick(N, (1024, 512, 256, 128))
    return pl.pallas_call(
        _mm2_kernel,
        grid=(M // tm, N // tn),
        in_specs=[pl.BlockSpec((tm, K1), lambda i, j: (i, 0)),
                  pl.BlockSpec((K1, tn), lambda i, j: (0, j)),
                  pl.BlockSpec((tm, K2), lambda i, j: (i, 0)),
                  pl.BlockSpec((K2, tn), lambda i, j: (0, j))],
        out_specs=pl.BlockSpec((tm, tn), lambda i, j: (i, j)),
        out_shape=jax.ShapeDtypeStruct((M, N), F32),
        compiler_params=_cparams(("parallel", "parallel")),
    )(x1, w1, x2, w2)


def _headmm_kernel(x_ref, w_ref, o_ref, *, transpose_w):
    if transpose_w:
        o_ref[...] = _dot_nt(x_ref[...], w_ref[...]).astype(o_ref.dtype)
    else:
        o_ref[...] = _dot(x_ref[...], w_ref[...]).astype(o_ref.dtype)


def head_mm(x, w, heads, kin, kout, *, transpose_w, row_start=0, rows=None):
    M = x.shape[0]
    rows = M - row_start if rows is None else rows
    tm = _pick(rows, (1024, 512, 256, 128))
    assert row_start % tm == 0
    r0 = row_start // tm
    wblock = (kout, kin) if transpose_w else (kin, kout)
    return pl.pallas_call(
        functools.partial(_headmm_kernel, transpose_w=transpose_w),
        grid=(rows // tm, heads),
        in_specs=[pl.BlockSpec((tm, kin), lambda i, h: (i + r0, h)),
                  pl.BlockSpec(wblock, lambda i, h: (0, h))],
        out_specs=pl.BlockSpec((tm, kout), lambda i, h: (i, h)),
        out_shape=jax.ShapeDtypeStruct((rows, heads * kout), BF16),
        compiler_params=_cparams(("parallel", "parallel")),
    )(x, w)


def _ln_kernel(x_ref, y_ref, g_ref, b_ref, of_ref, ob_ref, *, alpha):
    z = alpha * x_ref[...] + y_ref[...]
    mu = jnp.mean(z, axis=-1, keepdims=True)
    zc = z - mu
    var = jnp.mean(zc * zc, axis=-1, keepdims=True)
    out = zc * lax.rsqrt(var + LN_EPS) * g_ref[...] + b_ref[...]
    of_ref[...] = out
    ob_ref[...] = out.astype(BF16)


def residual_ln(x, y, g, b, alpha):
    T, D = x.shape
    tm = _pick(T, (256, 128))
    row = pl.BlockSpec((tm, D), lambda i: (i, 0))
    vec = pl.BlockSpec((1, D), lambda i: (0, 0))
    return pl.pallas_call(
        functools.partial(_ln_kernel, alpha=alpha),
        grid=(T // tm,),
        in_specs=[row, row, vec, vec],
        out_specs=[row, row],
        out_shape=[jax.ShapeDtypeStruct((T, D), F32), jax.ShapeDtypeStruct((T, D), BF16)],
        compiler_params=_cparams(("parallel",)),
    )(x, y, g.reshape(1, D), b.reshape(1, D))


def _gla_kernel(q_ref, k_ref, v_ref, gl_ref, r_ref, wg_ref, bg_ref, gn_ref, s0_ref,
                o_ref, s_ref, state, *, chunk, n_chunks, mxu_dtype):
    rb = pl.program_id(1)

    @pl.when(rb == 0)
    def _():
        state[...] = s0_ref[0]

    C = chunk
    row = lax.broadcasted_iota(jnp.int32, (C, C), 0)
    col = lax.broadcasted_iota(jnp.int32, (C, C), 1)
    causal = row >= col
    tri = causal.astype(F32)
    ones_cv = jnp.ones((C, GLA_DV), F32)
    hi = lax.Precision.HIGHEST
    md = mxu_dtype

    for c in range(n_chunks):
        sl = slice(c * C, (c + 1) * C)
        glow = gl_ref[sl, :].astype(BF16)
        outs, states = [], []
        for h in range(GLA_HEADS):
            kcols = slice(h * GLA_DK, (h + 1) * GLA_DK)
            vcols = slice(h * GLA_DV, (h + 1) * GLA_DV)
            qi = q_ref[sl, kcols] * (GLA_DK ** -0.5)
            ki = k_ref[sl, kcols]
            vi = v_ref[sl, vcols]
            z = _dot(glow, wg_ref[:, kcols]) + bg_ref[:, kcols]
            gi = (jnp.minimum(z, 0.0) - jnp.log(1.0 + jnp.exp(-jnp.abs(z)))) * (1.0 / GLA_TAU)
            b = _dot(tri, gi, precision=hi)
            b_last = b[C - 1:C, :]
            q_g = qi * jnp.exp(b)
            k_g = ki * jnp.exp(-b)
            att = jnp.where(causal, _dot_nt(q_g.astype(md), k_g.astype(md)), 0.0)
            s_prev = state[h]
            o = _dot(att.astype(md), vi.astype(md)) + _dot(q_g.astype(md), s_prev.astype(md))
            k_dec = ki * jnp.exp(b_last - b)
            decay = _dot_tn(gi, ones_cv, precision=hi)
            states.append(jnp.exp(decay) * s_prev + _dot_tn(k_dec.astype(md), vi.astype(md)))
            ms = jnp.mean(o * o, axis=-1, keepdims=True)
            on = o * lax.rsqrt(ms + RMS_EPS) * gn_ref[...]
            rr = r_ref[sl, vcols]
            outs.append((on * (rr * jax.nn.sigmoid(rr))).astype(o_ref.dtype))
        for h in range(GLA_HEADS):
            state[h] = states[h]
            o_ref[sl, h * GLA_DV:(h + 1) * GLA_DV] = outs[h]

    @pl.when(rb == pl.num_programs(1) - 1)
    def _():
        s_ref[0] = state[...]


def gla(hm, wg, bg, gn, s0, *, row_start, batch, seq):
    C = min(GLA_CHUNK, seq)
    assert seq % C == 0
    n_chunks = _pick(seq // C, (4, 2, 1))
    R = C * n_chunks
    nrb = seq // R
    assert row_start % R == 0
    r0 = row_start // R
    H = GLA_HEADS
    QW, VW = H * GLA_DK, H * GLA_DV
    glow_blk = (2 * QW + 2 * VW + 2 * GMLP_GROUPS * GMLP_DG) // LANES
    rowmap = lambda blk: (lambda b, rb: (r0 + b * nrb + rb, blk))
    mxu_dtype = BF16 if C >= 16 else F32
    out_dtype = BF16 if R >= 16 else F32
    kernel = functools.partial(_gla_kernel, chunk=C, n_chunks=n_chunks, mxu_dtype=mxu_dtype)
    return pl.pallas_call(
        kernel,
        grid=(batch, nrb),
        in_specs=[
            pl.BlockSpec((R, QW), rowmap(0)),
            pl.BlockSpec((R, QW), rowmap(1)),
            pl.BlockSpec((R, VW), rowmap(2 * QW // VW)),
            pl.BlockSpec((R, LANES), rowmap(glow_blk)),
            pl.BlockSpec((R, VW), rowmap(2 * QW // VW + 1)),
            pl.BlockSpec((LANES, QW), lambda b, rb: (0, 0)),
            pl.BlockSpec((1, QW), lambda b, rb: (0, 0)),
            pl.BlockSpec((1, GLA_DV), lambda b, rb: (0, 0)),
            pl.BlockSpec((1, H, GLA_DK, GLA_DV), lambda b, rb: (b, 0, 0, 0)),
        ],
        out_specs=[
            pl.BlockSpec((R, VW), lambda b, rb: (b * nrb + rb, 0)),
            pl.BlockSpec((1, H, GLA_DK, GLA_DV), lambda b, rb: (b, 0, 0, 0)),
        ],
        out_shape=[jax.ShapeDtypeStruct((batch * seq, VW), out_dtype),
                   jax.ShapeDtypeStruct((batch, H, GLA_DK, GLA_DV), F32)],
        scratch_shapes=[pltpu.VMEM((H, GLA_DK, GLA_DV), F32)],
        compiler_params=_cparams(("parallel", "arbitrary")),
    )(hm, hm, hm, hm, hm, wg, bg, gn, s0)


def _gmlp_kernel(gu_ref, gv_ref, lng_ref, lnb_ref, w_ref, b_ref, m_ref, *gvo_ref):
    u = _gelu(gu_ref[...])
    v = _gelu(gv_ref[...])
    mu = jnp.mean(v, axis=-1, keepdims=True)
    vc = v - mu
    var = jnp.mean(vc * vc, axis=-1, keepdims=True)
    vn = vc * lax.rsqrt(var + LN_EPS) * lng_ref[0] + lnb_ref[0]
    s = _dot(w_ref[0], vn.astype(BF16)) + b_ref[0]
    m_ref[...] = (u * s).astype(m_ref.dtype)
    if gvo_ref:
        gvo_ref[0][...] = vn


def gmlp(hm, lng, lnb, w_eff, b_eff, *, row_start, rows, emit_v):
    G = GMLP_GROUPS
    R = GMLP_CHUNK
    assert rows % R == 0 and row_start % R == 0
    r0 = row_start // R
    ublk = (2 * GLA_HEADS * GLA_DK + 2 * GLA_HEADS * GLA_DV) // GMLP_DG
    vblk = ublk + G
    out_block = pl.BlockSpec((R, GMLP_DG), lambda i, g: (i, g))
    out_specs = [out_block]
    out_shape = [jax.ShapeDtypeStruct((rows, G * GMLP_DG), BF16)]
    if emit_v:
        out_specs.append(out_block)
        out_shape.append(jax.ShapeDtypeStruct((rows, G * GMLP_DG), F32))
    return pl.pallas_call(
        _gmlp_kernel,
        grid=(rows // R, G),
        in_specs=[
            pl.BlockSpec((R, GMLP_DG), lambda i, g: (r0 + i, ublk + g)),
            pl.BlockSpec((R, GMLP_DG), lambda i, g: (r0 + i, vblk + g)),
            pl.BlockSpec((1, 1, GMLP_DG), lambda i, g: (g, 0, 0)),
            pl.BlockSpec((1, 1, GMLP_DG), lambda i, g: (g, 0, 0)),
            pl.BlockSpec((1, R, R), lambda i, g: (g, 0, 0)),
            pl.BlockSpec((1, R, 1), lambda i, g: (g, 0, 0)),
        ],
        out_specs=out_specs,
        out_shape=out_shape,
        compiler_params=_cparams(("parallel", "parallel")),
    )(hm, hm, lng.reshape(G, 1, GMLP_DG), lnb.reshape(G, 1, GMLP_DG), w_eff, b_eff.reshape(G, R, 1))


def _mla_post_kernel(h_ref, cs_ref, qg_ref, kg_ref, cq_ref, ckv_ref, ckvb_ref, kpe_ref, kpeb_ref):
    cq = h_ref[:, :Q_LORA]
    ckv = h_ref[:, Q_LORA:Q_LORA + KV_LORA]
    cqn = cq * lax.rsqrt(jnp.mean(cq * cq, axis=-1, keepdims=True) + RMS_EPS) * qg_ref[...]
    ckvn = ckv * lax.rsqrt(jnp.mean(ckv * ckv, axis=-1, keepdims=True) + RMS_EPS) * kg_ref[...]
    cq_ref[...] = cqn.astype(BF16)
    ckv_ref[...] = ckvn
    ckvb_ref[...] = ckvn.astype(BF16)
    t = h_ref[:, Q_LORA + KV_LORA:] * cs_ref[...]
    rot = t + pltpu.roll(t, QK_ROPE, 1)
    lane = lax.broadcasted_iota(jnp.int32, rot.shape, 1)
    rot = jnp.where(lane < QK_ROPE, rot, 0.0)
    kpe_ref[...] = rot
    kpeb_ref[...] = rot.astype(BF16)


def mla_post(h, cs, q_norm_g, kv_norm_g):
    T = h.shape[0]
    tm = _pick(T, (512, 256, 128))
    W = h.shape[1]
    return pl.pallas_call(
        _mla_post_kernel,
        grid=(T // tm,),
        in_specs=[pl.BlockSpec((tm, W), lambda i: (i, 0)),
                  pl.BlockSpec((tm, LANES), lambda i: (i, 0)),
                  pl.BlockSpec((1, Q_LORA), lambda i: (0, 0)),
                  pl.BlockSpec((1, KV_LORA), lambda i: (0, 0))],
        out_specs=[pl.BlockSpec((tm, Q_LORA), lambda i: (i, 0)),
                   pl.BlockSpec((tm, KV_LORA), lambda i: (i, 0)),
                   pl.BlockSpec((tm, KV_LORA), lambda i: (i, 0)),
                   pl.BlockSpec((tm, LANES), lambda i: (i, 0)),
                   pl.BlockSpec((tm, LANES), lambda i: (i, 0))],
        out_shape=[jax.ShapeDtypeStruct((T, Q_LORA), BF16),
                   jax.ShapeDtypeStruct((T, KV_LORA), F32),
                   jax.ShapeDtypeStruct((T, KV_LORA), BF16),
                   jax.ShapeDtypeStruct((T, LANES), F32),
                   jax.ShapeDtypeStruct((T, LANES), BF16)],
        compiler_params=_cparams(("parallel",)),
    )(h, cs, q_norm_g.reshape(1, Q_LORA), kv_norm_g.reshape(1, KV_LORA))


def _qpe_kernel(cq_ref, w_ref, wsw_ref, cs_ref, o_ref):
    a = _dot(cq_ref[...], w_ref[...])
    b = _dot(cq_ref[...], wsw_ref[...])
    cos = cs_ref[:, :LANES]
    sin = cs_ref[:, LANES:]
    for h in range(MLA_HEADS):
        sl = slice(h * LANES, (h + 1) * LANES)
        o_ref[:, sl] = ((a[:, sl] * cos + b[:, sl] * sin) * MLA_SCALE).astype(o_ref.dtype)


def q_rope(cq, w_pe, w_pe_sw, cs2):
    T, K = cq.shape
    N = w_pe.shape[1]
    tm = _pick(T, (512, 256, 128))
    return pl.pallas_call(
        _qpe_kernel,
        grid=(T // tm,),
        in_specs=[pl.BlockSpec((tm, K), lambda i: (i, 0)),
                  pl.BlockSpec((K, N), lambda i: (0, 0)),
                  pl.BlockSpec((K, N), lambda i: (0, 0)),
                  pl.BlockSpec((tm, 2 * LANES), lambda i: (i, 0))],
        out_specs=pl.BlockSpec((tm, N), lambda i: (i, 0)),
        out_shape=jax.ShapeDtypeStruct((T, N), BF16),
        compiler_params=_cparams(("parallel",)),
    )(cq, w_pe, w_pe_sw, cs2)


def _flash_kernel(qn_ref, qp_ref, kn_ref, kp_ref, vt_ref, o_ref, m_scr, l_scr, acc_scr, *, tq):
    qi = pl.program_id(2)
    q = jnp.concatenate([qn_ref[...], qp_ref[...]], axis=1)
    m_scr[...] = jnp.full_like(m_scr, NEG_INF)
    l_scr[...] = jnp.zeros_like(l_scr)
    acc_scr[...] = jnp.zeros_like(acc_scr)

    def block(ki, n_blocks, masked):
        start = pl.multiple_of(ki * tq, tq)
        width = n_blocks * tq
        k = jnp.concatenate([kn_ref[pl.ds(start, width), :], kp_ref[pl.ds(start, width), :]], axis=1)
        st = _dot_nt(k, q)
        if masked:
            key = lax.broadcasted_iota(jnp.int32, st.shape, 0)
            qry = lax.broadcasted_iota(jnp.int32, st.shape, 1)
            st = jnp.where(key <= qry, st, NEG_INF)
        m_prev = m_scr[...]
        m_new = jnp.maximum(m_prev, jnp.max(st, axis=0, keepdims=True))
        alpha = jnp.exp(m_prev - m_new)
        pf = jnp.exp(st - m_new)
        p = pf.astype(BF16)
        pv = _dot(vt_ref[ki], p[:tq])
        for i in range(1, n_blocks):
            pv = pv + _dot(vt_ref[ki + i], p[i * tq:(i + 1) * tq])
        l_scr[...] = alpha * l_scr[...] + jnp.sum(pf, axis=0, keepdims=True)
        acc_scr[...] = alpha * acc_scr[...] + pv
        m_scr[...] = m_new

    def body(kp, carry):
        block(2 * kp, 2, False)
        return carry

    lax.fori_loop(0, qi // 2, body, 0)

    @pl.when(qi % 2 == 1)
    def _():
        block(qi - 1, 1, False)

    block(qi, 1, True)
    o_ref[...] = (acc_scr[...] / l_scr[...]).T.astype(o_ref.dtype)


def _vt_kernel(w_ref, c_ref, o_ref):
    o_ref[0] = _dot_nt(w_ref[...], c_ref[...]).astype(o_ref.dtype)


def value_blocks_t(ckv, w_uv_t, *, rows, tk):
    K = ckv.shape[1]
    N = w_uv_t.shape[0]
    return pl.pallas_call(
        _vt_kernel,
        grid=(rows // tk,),
        in_specs=[pl.BlockSpec((N, K), lambda i: (0, 0)),
                  pl.BlockSpec((tk, K), lambda i: (i, 0))],
        out_specs=pl.BlockSpec((1, N, tk), lambda i: (i, 0, 0)),
        out_shape=jax.ShapeDtypeStruct((rows // tk, N, tk), BF16),
        compiler_params=_cparams(("parallel",)),
    )(w_uv_t, ckv)


def mla_prompt_attention(qn, qp, kn, kp, vt, *, batch, seq, tq):
    nq = seq // tq
    H = MLA_HEADS
    return pl.pallas_call(
        functools.partial(_flash_kernel, tq=tq),
        grid=(batch, H, nq),
        in_specs=[pl.BlockSpec((tq, LANES), lambda b, h, qi: (b * nq + qi, h)),
                  pl.BlockSpec((tq, LANES), lambda b, h, qi: (b * nq + qi, h)),
                  pl.BlockSpec((seq, LANES), lambda b, h, qi: (b, h)),
                  pl.BlockSpec((seq, LANES), lambda b, h, qi: (b, 0)),
                  pl.BlockSpec((nq, V_HEAD, tq), lambda b, h, qi: (b, h, 0))],
        out_specs=pl.BlockSpec((tq, LANES), lambda b, h, qi: (b * nq + qi, h)),
        out_shape=jax.ShapeDtypeStruct((batch * seq, H * V_HEAD), BF16),
        scratch_shapes=[pltpu.VMEM((1, tq), F32), pltpu.VMEM((1, tq), F32), pltpu.VMEM((V_HEAD, tq), F32)],
        compiler_params=_cparams(("parallel", "parallel", "arbitrary")),
    )(qn, qp, kn, kp, vt)


def _paged_kernel(pt_ref, ql_ref, qp_ref, cn_ref, kn_ref, *rest, pages_per_step, dec_seq):
    P = pages_per_step
    ckv_refs = rest[:P]
    kpe_refs = rest[P:2 * P]
    o_ref = rest[2 * P]
    m_scr, l_scr, acc_scr = rest[2 * P + 1:]
    j = pl.program_id(1)
    rows = dec_seq * MLA_HEADS

    @pl.when(j == 0)
    def _():
        m_scr[...] = jnp.full_like(m_scr, NEG_INF)
        l_scr[...] = jnp.zeros_like(l_scr)
        acc_scr[...] = jnp.zeros_like(acc_scr)

    ql = ql_ref[...].reshape(rows, KV_LORA)
    qp = qp_ref[...].reshape(rows, LANES)
    qp_rope = qp[:, :QK_ROPE]

    kc = jnp.concatenate([r[0].astype(BF16) for r in ckv_refs], axis=0)
    kk_t = jnp.concatenate([r[0].astype(BF16) for r in kpe_refs], axis=1)
    s = _dot_nt(ql, kc) + _dot(qp_rope, kk_t)
    m_prev = m_scr[...]
    m_new = jnp.maximum(m_prev, jnp.max(s, axis=1, keepdims=True))
    alpha = jnp.exp(m_prev - m_new)
    p = jnp.exp(s - m_new)
    l_scr[...] = alpha * l_scr[...] + jnp.sum(p, axis=1, keepdims=True)
    acc_scr[...] = alpha * acc_scr[...] + _dot(p.astype(BF16), kc)
    m_scr[...] = m_new

    @pl.when(j == pl.num_programs(1) - 1)
    def _():
        cn = cn_ref[...]
        kn = kn_ref[...]
        sn = _dot_nt(ql.astype(F32), cn) + _dot_nt(qp.astype(F32), kn)
        t_row = lax.broadcasted_iota(jnp.int32, sn.shape, 0) // MLA_HEADS
        t_col = lax.broadcasted_iota(jnp.int32, sn.shape, 1)
        sn = jnp.where(t_col <= t_row, sn, NEG_INF)
        m_prev = m_scr[...]
        m_new = jnp.maximum(m_prev, jnp.max(sn, axis=1, keepdims=True))
        alpha = jnp.exp(m_prev - m_new)
        pn = jnp.exp(sn - m_new)
        l = alpha * l_scr[...] + jnp.sum(pn, axis=1, keepdims=True)
        acc = alpha * acc_scr[...] + _dot(pn, cn)
        o_ref[...] = (acc / l).reshape(dec_seq, MLA_HEADS, KV_LORA).astype(o_ref.dtype)


def mla_sample_attention(page_table, q_lat, qp3, ckv_f32, kpe_f32, cache_ckv, cache_kpe, page_base, *, row_start, dec_seq):
    B, n_pages = page_table.shape
    page = cache_ckv.shape[1]
    P = _pick(n_pages, (16, 8, 4, 2, 1))
    nsteps = n_pages // P
    assert row_start % dec_seq == 0
    r0 = row_start // dec_seq
    H = MLA_HEADS

    def page_map(p):
        return lambda b, j, pt: (page_base + pt[b * n_pages + j * P + p], 0, 0)

    in_specs = [
        pl.BlockSpec((dec_seq, H, KV_LORA), lambda b, j, pt: (b, 0, 0)),
        pl.BlockSpec((dec_seq, H, LANES), lambda b, j, pt: (r0 + b, 0, 0)),
        pl.BlockSpec((dec_seq, KV_LORA), lambda b, j, pt: (r0 + b, 0)),
        pl.BlockSpec((dec_seq, LANES), lambda b, j, pt: (r0 + b, 0)),
    ]
    in_specs += [pl.BlockSpec((1, page, KV_LORA), page_map(p)) for p in range(P)]
    in_specs += [pl.BlockSpec((1, QK_ROPE, page), page_map(p)) for p in range(P)]
    rows = dec_seq * H
    grid_spec = pltpu.PrefetchScalarGridSpec(
        num_scalar_prefetch=1,
        grid=(B, nsteps),
        in_specs=in_specs,
        out_specs=pl.BlockSpec((dec_seq, H, KV_LORA), lambda b, j, pt: (b, 0, 0)),
        scratch_shapes=[pltpu.VMEM((rows, 1), F32), pltpu.VMEM((rows, 1), F32), pltpu.VMEM((rows, KV_LORA), F32)],
    )
    return pl.pallas_call(
        functools.partial(_paged_kernel, pages_per_step=P, dec_seq=dec_seq),
        grid_spec=grid_spec,
        out_shape=jax.ShapeDtypeStruct((B * dec_seq, H, KV_LORA), BF16),
        compiler_params=_cparams(("parallel", "arbitrary")),
    )(page_table.reshape(-1), q_lat, qp3, ckv_f32, kpe_f32, *([cache_ckv] * P), *([cache_kpe] * P))


RANK_BASE = 1e30
RANK_STEP = 1e28


def _top_values(s, k, with_rank=False):
    out = []
    for r in range(k):
        m = jnp.max(s, axis=0, keepdims=True)
        out.append(m)
        s = jnp.where(s == m, -(RANK_BASE + r * RANK_STEP) if with_rank else -jnp.inf, s)
    if not with_rank:
        return out
    rank = jnp.where(s <= -RANK_BASE, jnp.round((-s - RANK_BASE) * (1.0 / RANK_STEP)), float(k))
    return out, rank


def _peer_topk_kernel(q_ref, keys_ref, k1_ref, c1_ref, r2_ref, e2_ref):
    K = PEER_TOPK
    for h in range(PEER_HEADS):
        qa = q_ref[:, (2 * h) * LANES:(2 * h + 1) * LANES]
        qb = q_ref[:, (2 * h + 1) * LANES:(2 * h + 2) * LANES]
        sa = _dot_nt(keys_ref[2 * h], qa)
        sb = _dot_nt(keys_ref[2 * h + 1], qb)
        ta, rank_a = _top_values(sa, K, with_rank=True)
        tb, rank_b = _top_values(sb, K, with_rank=True)
        ta_all = jnp.concatenate(ta, axis=0)
        tb_all = jnp.concatenate(tb, axis=0)
        half = K // 2
        rows_a0 = ta[0] + tb_all
        rows_lo = [ta[a] + tb_all[:half] for a in range(1, half)]
        rows_hi = ta_all[half:] + tb[0]
        best = _top_values(jnp.concatenate([rows_a0] + rows_lo + [rows_hi], axis=0), K)
        tau = best[K - 1]
        z = jnp.zeros_like(best[0])
        for r in range(K):
            z = z + jnp.exp(best[r] - best[0])
        cnt = [jnp.sum((rows >= tau).astype(F32), axis=0, keepdims=True) for rows in [rows_a0] + rows_lo]
        cnt_hi = (rows_hi >= tau).astype(F32)
        cnt += [cnt_hi[a:a + 1] for a in range(K - half)]
        k1 = jnp.zeros_like(sa)
        for a in range(K):
            k1 = jnp.where(rank_a == float(a), cnt[a], k1)
        k1_ref[h] = k1
        r2_ref[h] = rank_b.astype(r2_ref.dtype)
        c1_ref[h] = jnp.exp(sa - ta[0]) / z
        e2_ref[h] = jnp.exp(sb - tb[0]).astype(e2_ref.dtype)


def peer_topk(q, keys):
    T = q.shape[0]
    TT = _pick(T, (512, 256, 128))
    blk = pl.BlockSpec((PEER_HEADS, PEER_NKEYS, TT), lambda i: (0, 0, i))
    f32 = jax.ShapeDtypeStruct((PEER_HEADS, PEER_NKEYS, T), F32)
    b16 = jax.ShapeDtypeStruct((PEER_HEADS, PEER_NKEYS, T), BF16)
    return pl.pallas_call(
        _peer_topk_kernel,
        grid=(T // TT,),
        in_specs=[pl.BlockSpec((TT, q.shape[1]), lambda i: (i, 0)),
                  pl.BlockSpec(keys.shape, lambda i: (0, 0, 0))],
        out_specs=[blk, blk, blk, blk],
        out_shape=[f32, f32, b16, b16],
        compiler_params=_cparams(("parallel",)),
    )(q, keys)


PEER_TE = 8 * PEER_NKEYS


def _peer_dense_kernel(x_ref, u_ref, vt_ref, k1_ref, c1_ref, r2_ref, e2_ref, o_ref, acc_scr, a_scr):
    j = pl.program_id(1)

    @pl.when(j == 0)
    def _():
        acc_scr[...] = jnp.zeros_like(acc_scr)

    x = x_ref[...]
    zero = jnp.zeros((), BF16)
    gps = 2
    for g in range(PEER_TE // PEER_NKEYS):
        if g % gps == 0:
            slab = slice(g * PEER_NKEYS, (g + gps) * PEER_NKEYS)
            ht = _dot_nt(u_ref[0, slab, :], x)
        gate = None
        for h in range(PEER_HEADS):
            k1 = k1_ref[h, g:g + 1, :].astype(BF16)
            c1 = c1_ref[h, g:g + 1, :].astype(BF16)
            term = jnp.where(r2_ref[h] < k1, e2_ref[h] * c1, zero)
            gate = term if gate is None else gate + term
        hs = ht[(g % gps) * PEER_NKEYS:(g % gps + 1) * PEER_NKEYS, :]
        a_scr[g * PEER_NKEYS:(g + 1) * PEER_NKEYS, :] = _gelu_sigmoid(hs.astype(BF16)) * gate
    acc_scr[...] += _dot(vt_ref[0], a_scr[...])

    @pl.when(j == pl.num_programs(1) - 1)
    def _():
        o_ref[...] = acc_scr[...].T


def peer_dense(x, u, vt, layer, k1, c1, r2, e2):
    T, D = x.shape
    E = u.shape[1]
    TT = _pick(T, (512, 256, 128))
    TE = PEER_TE
    assert E % TE == 0 and E == PEER_NKEYS * PEER_NKEYS
    full = pl.BlockSpec((PEER_HEADS, PEER_NKEYS, TT), lambda i, j: (0, 0, i))
    rows8 = pl.BlockSpec((PEER_HEADS, SUBLANES, TT), lambda i, j: (0, j, i))
    return pl.pallas_call(
        _peer_dense_kernel,
        grid=(T // TT, E // TE),
        in_specs=[pl.BlockSpec((TT, D), lambda i, j: (i, 0)),
                  pl.BlockSpec((1, TE, D), lambda i, j: (layer, j, 0)),
                  pl.BlockSpec((1, D, TE), lambda i, j: (layer, 0, j)),
                  rows8, rows8, full, full],
        out_specs=pl.BlockSpec((TT, D), lambda i, j: (i, 0)),
        out_shape=jax.ShapeDtypeStruct((T, D), F32),
        scratch_shapes=[pltpu.VMEM((D, TT), F32), pltpu.VMEM((TE, TT), BF16)],
        compiler_params=_cparams(("parallel", "arbitrary")),
    )(x, u, vt, k1, c1, r2, e2)


def _cast_transpose_kernel(v_ref, o_ref):
    o_ref[0] = v_ref[0].T.astype(o_ref.dtype)


def cast_transpose(v):
    L, E, D = v.shape
    te = _pick(E, (512, 256, 128))
    return pl.pallas_call(
        _cast_transpose_kernel,
        grid=(L, E // te),
        in_specs=[pl.BlockSpec((1, te, D), lambda l, j: (l, j, 0))],
        out_specs=pl.BlockSpec((1, D, te), lambda l, j: (l, 0, j)),
        out_shape=jax.ShapeDtypeStruct((L, D, E), BF16),
        compiler_params=_cparams(("parallel", "parallel")),
    )(v)


def _rope_swap_cols(w):
    half = QK_ROPE // 2
    return jnp.concatenate([-w[..., half:], w[..., :half]], axis=-1)


def _prep_even(w_in, w_gate_up, w_out):
    qk = 2 * GLA_HEADS * GLA_DK
    vw = GLA_HEADS * GLA_DV
    g0 = qk + vw
    g1 = g0 + GLA_GATE_RANK
    K = w_in.shape[0]
    w_perm = jnp.concatenate(
        [w_in[:, :g0], w_in[:, g1:], w_in[:, g0:g1], jnp.zeros((K, GLOW_PAD - GLA_GATE_RANK), w_in.dtype)], axis=1)
    wg = jnp.concatenate([w_gate_up, jnp.zeros((LANES - GLA_GATE_RANK, w_gate_up.shape[1]), w_gate_up.dtype)], axis=0)
    return w_perm.astype(BF16), wg.astype(BF16), w_out[:vw].astype(BF16), w_out[vw:].astype(BF16)


def _prep_gmlp_spatial(w_sp, b_sp, seq):
    C = min(GMLP_CHUNK, seq)
    assert seq % C == 0 and GMLP_CHUNK % C == 0
    w = jnp.where(jnp.tril(jnp.ones((C, C), bool)), w_sp[:, :C, :C], 0.0)
    b = b_sp[:, :C]
    rep = GMLP_CHUNK // C
    if rep > 1:
        eye = jnp.eye(rep, dtype=w.dtype)
        w = jnp.einsum('ab,gij->gaibj', eye, w).reshape(w.shape[0], GMLP_CHUNK, GMLP_CHUNK)
        b = jnp.tile(b, (1, rep))
    return w.astype(BF16), b


def _prep_odd(w_in, w_uq, w_uk, w_uv, w_o):
    kpe_w = w_in[:, Q_LORA + KV_LORA:]
    w_in_p = jnp.concatenate([w_in, _rope_swap_cols(kpe_w)], axis=1).astype(BF16)
    w_n = w_uq[:, :, :QK_NOPE].reshape(Q_LORA, MLA_HEADS * QK_NOPE).astype(BF16)
    pe = w_uq[:, :, QK_NOPE:]
    zpad = jnp.zeros((Q_LORA, MLA_HEADS, LANES - QK_ROPE), w_uq.dtype)
    w_pe = jnp.concatenate([pe, zpad], axis=-1).reshape(Q_LORA, MLA_HEADS * LANES).astype(BF16)
    w_pe_sw = jnp.concatenate([_rope_swap_cols(pe), zpad], axis=-1).reshape(Q_LORA, MLA_HEADS * LANES).astype(BF16)
    w_uk2 = w_uk.reshape(KV_LORA, MLA_HEADS * QK_NOPE).astype(BF16)
    w_uv2 = w_uv.reshape(KV_LORA, MLA_HEADS * V_HEAD).astype(BF16)
    return w_in_p, w_n, w_pe, w_pe_sw, w_uk2, w_uv2, w_o.astype(BF16)


def _rope_tables(pos):
    inv = ROPE_THETA ** (-jnp.arange(0, QK_ROPE, 2, dtype=F32) / QK_ROPE)
    ang = pos.astype(F32)[:, None] * inv[None, :]
    cos, sin = jnp.cos(ang), jnp.sin(ang)
    z = jnp.zeros((pos.shape[0], LANES - QK_ROPE), F32)
    cs_k = jnp.concatenate([cos, cos, sin, sin], axis=1)
    cs_q = jnp.concatenate([cos, cos, z, sin, sin, z], axis=1)
    return cs_k, cs_q


def kernel(x_prompt, x_sample, state_gla, cache_ckv, cache_kpe, page_table, w_in_even, w_gate_up, b_gate, gla_norm_g, gmlp_ln_g, gmlp_ln_b, w_spatial, b_spatial, w_out_even, w_in_odd, q_norm_g, kv_norm_g, w_uq, w_uk, w_uv, w_o_odd, ln_mix_g, ln_mix_b, ln_ffn_g, ln_ffn_b, w_peer_q, peer_keys, peer_u, peer_v):
    B, L, D = x_prompt.shape
    Bs, Ls, _ = x_sample.shape
    Tp, Ts = B * L, Bs * Ls
    T = Tp + Ts
    depth = ln_mix_g.shape[0]
    n_pool, page = cache_ckv.shape[1], cache_ckv.shape[2]
    past_len = page_table.shape[1] * page
    dn_alpha = (2.0 * depth) ** 0.25

    x = jnp.concatenate([x_prompt.reshape(Tp, D), x_sample.reshape(Ts, D)], axis=0)
    xb = x.astype(BF16)

    pos = jnp.concatenate([jnp.tile(jnp.arange(L, dtype=jnp.int32), B),
                           jnp.tile(past_len + jnp.arange(Ls, dtype=jnp.int32), Bs)])
    cs_k, cs_q = _rope_tables(pos)

    peer_u_b = peer_u.astype(BF16)
    peer_vt_b = cast_transpose(peer_v)
    cache_ckv2 = cache_ckv.reshape(-1, page, KV_LORA)
    cache_kpe2 = jnp.swapaxes(cache_kpe, 2, 3).reshape(-1, QK_ROPE, page)
    zeros_state = jnp.zeros((B, GLA_HEADS, GLA_DK, GLA_DV), F32)

    gla_p, gla_s, gmlp_s, ckv_out, kpe_out = [], [], [], [], []
    for l in range(depth):
        i = l // 2
        if l % 2 == 0:
            w_in, wg, w_out_o, w_out_m = _prep_even(w_in_even[i], w_gate_up[i], w_out_even[i])
            hm = mm(xb, w_in, F32)
            bg = b_gate[i].reshape(1, -1)
            gn = gla_norm_g[i].reshape(1, -1)
            o_p, s_p = gla(hm, wg, bg, gn, zeros_state, row_start=0, batch=B, seq=L)
            o_s, s_s = gla(hm, wg, bg, gn, state_gla[i], row_start=Tp, batch=Bs, seq=Ls)
            wsp_p, bsp_p = _prep_gmlp_spatial(w_spatial[i], b_spatial[i], L)
            wsp_s, bsp_s = _prep_gmlp_spatial(w_spatial[i], b_spatial[i], Ls)
            (m_p,) = gmlp(hm, gmlp_ln_g[i], gmlp_ln_b[i], wsp_p, bsp_p, row_start=0, rows=Tp, emit_v=False)
            m_s, gv_s = gmlp(hm, gmlp_ln_g[i], gmlp_ln_b[i], wsp_s, bsp_s, row_start=Tp, rows=Ts, emit_v=True)
            y = mm2(jnp.concatenate([o_p, o_s.astype(BF16)], axis=0), w_out_o, jnp.concatenate([m_p, m_s], axis=0), w_out_m)
            gla_p.append(s_p)
            gla_s.append(s_s)
            gmlp_s.append(gv_s.reshape(Bs, Ls, GMLP_GROUPS, GMLP_DG))
        else:
            w_in, w_n, w_pe, w_pe_sw, w_uk2, w_uv2, w_o = _prep_odd(w_in_odd[i], w_uq[i], w_uk[i], w_uv[i], w_o_odd[i])
            h = mm(xb, w_in, F32)
            cq, ckv, ckv_b, kpe, kpe_b = mla_post(h, cs_k, q_norm_g[i], kv_norm_g[i])
            qn = mm(cq, w_n, BF16, scale=MLA_SCALE)
            qp = q_rope(cq, w_pe, w_pe_sw, cs_q)
            kn = mm(ckv_b, w_uk2, BF16, rows=Tp)
            tq = _pick(L, (512, 256, 128))
            vt = value_blocks_t(ckv_b, w_uv2.T, rows=Tp, tk=tq)
            o_p = mla_prompt_attention(qn, qp, kn, kpe_b, vt, batch=B, seq=L, tq=tq)
            q_lat = head_mm(qn, w_uk2, MLA_HEADS, QK_NOPE, KV_LORA, transpose_w=True, row_start=Tp, rows=Ts)
            o_lat = mla_sample_attention(page_table, q_lat.reshape(Ts, MLA_HEADS, KV_LORA),
                                         qp.reshape(T, MLA_HEADS, LANES), ckv, kpe, cache_ckv2, cache_kpe2,
                                         i * n_pool, row_start=Tp, dec_seq=Ls)
            o_s = head_mm(o_lat.reshape(Ts, MLA_HEADS * KV_LORA), w_uv2, MLA_HEADS, KV_LORA, V_HEAD, transpose_w=False)
            y = mm(jnp.concatenate([o_p, o_s], axis=0), w_o, F32)
            ckv_out.append(ckv)
            kpe_out.append(kpe[:, :QK_ROPE])
        x, xb = residual_ln(x, y, ln_mix_g[l], ln_mix_b[l], dn_alpha)
        q = mm(xb, w_peer_q[l].astype(BF16), BF16)
        keys = peer_keys[l].reshape(PEER_HEADS * 2, PEER_NKEYS, -1).astype(BF16)
        k1, c1, r2, e2 = peer_topk(q, keys)
        y = peer_dense(xb, peer_u_b, peer_vt_b, l, k1, c1, r2, e2)
        x, xb = residual_ln(x, y, ln_ffn_g[l], ln_ffn_b[l], dn_alpha)

    def split(rows_list, width):
        p = jnp.stack([r[:Tp].reshape(B, L, width) for r in rows_list])
        s = jnp.stack([r[Tp:].reshape(Bs, Ls, width) for r in rows_list])
        return p, s

    ckv_p, ckv_s = split(ckv_out, KV_LORA)
    kpe_p, kpe_s = split(kpe_out, QK_ROPE)
    return (x[:Tp].reshape(B, L, D), x[Tp:].reshape(Bs, Ls, D),
            jnp.stack(gla_p), jnp.stack(gla_s), jnp.stack(gmlp_s),
            ckv_p, kpe_p, ckv_s, kpe_s)
```

```python
import functools

import jax
import jax.numpy as jnp
from jax import lax
from jax.experimental import pallas as pl
from jax.experimental.pallas import tpu as pltpu

F32 = jnp.float32
BF16 = jnp.bfloat16

GLA_HEADS = 4
GLA_DK = 128
GLA_DV = 256
GLA_GATE_RANK = 16
GLA_TAU = 16.0
GLA_CHUNK = 64
GMLP_GROUPS = 4
GMLP_DG = 256
GMLP_CHUNK = 128
MLA_HEADS = 16
Q_LORA = 512
KV_LORA = 512
QK_NOPE = 128
QK_ROPE = 64
V_HEAD = 128
ROPE_THETA = 10000.0
MLA_SCALE = (QK_NOPE + QK_ROPE) ** -0.5
PEER_HEADS = 8
PEER_NKEYS = 128
PEER_TOPK = 16
LN_EPS = 1e-5
RMS_EPS = 1e-6
NEG_INF = -1e30

LANES = 128
SUBLANES = 8
VMEM_LIMIT_BYTES = 56 * 1024 * 1024

GLOW_PAD = 256


def _cparams(sem):
    return pltpu.CompilerParams(dimension_semantics=sem, vmem_limit_bytes=VMEM_LIMIT_BYTES)


def _pick(n, prefs):
    for p in prefs:
        if n % p == 0:
            return p
    raise ValueError(f"no tile in {prefs} divides {n}")


def _gelu(x):
    return 0.5 * x * (1.0 + jnp.tanh(0.7978845608028654 * (x + 0.044715 * (x * x * x))))


def _gelu_sigmoid(x):
    c = 2.0 * 0.7978845608028654
    neg2y = x * (x * x * (-c * 0.044715) - c)
    return x / (1.0 + jnp.exp(neg2y))


def _dot(a, b, **kw):
    return jnp.dot(a, b, preferred_element_type=F32, **kw)


def _dot_nt(a, b, **kw):
    return lax.dot_general(a, b, (((1,), (1,)), ((), ())), preferred_element_type=F32, **kw)


def _dot_tn(a, b, **kw):
    return lax.dot_general(a, b, (((0,), (0,)), ((), ())), preferred_element_type=F32, **kw)


def _mm_kernel(x_ref, w_ref, o_ref, *, scale):
    acc = _dot(x_ref[...], w_ref[...])
    if scale != 1.0:
        acc = acc * scale
    o_ref[...] = acc.astype(o_ref.dtype)


def mm(x, w, out_dtype, *, rows=None, row_start=0, scale=1.0, tn_prefs=(1792, 1024, 768, 512, 384, 256, 128)):
    M, K = x.shape
    N = w.shape[1]
    rows = M - row_start if rows is None else rows
    tm = _pick(rows, (1024, 512, 256, 128))
    assert row_start % tm == 0
    tn = _pick(N, tn_prefs)
    r0 = row_start // tm
    return pl.pallas_call(
        functools.partial(_mm_kernel, scale=scale),
        grid=(rows // tm, N // tn),
        in_specs=[pl.BlockSpec((tm, K), lambda i, j: (i + r0, 0)),
                  pl.BlockSpec((K, tn), lambda i, j: (0, j))],
        out_specs=pl.BlockSpec((tm, tn), lambda i, j: (i, j)),
        out_shape=jax.ShapeDtypeStruct((rows, N), out_dtype),
        compiler_params=_cparams(("parallel", "parallel")),
    )(x, w)


def _mm2_kernel(x1_ref, w1_ref, x2_ref, w2_ref, o_ref):
    o_ref[...] = _dot(x1_ref[...], w1_ref[...]) + _dot(x2_ref[...], w2_ref[...])


def mm2(x1, w1, x2, w2):
    M, K1 = x1.shape
    K2 = x2.shape[1]
    N = w1.shape[1]
    tm = _pick(M, (1024, 512, 256, 128))
    tn = _pick(N, (1024, 512, 256, 128))
    return pl.pallas_call(
        _mm2_kernel,
        grid=(M // tm, N // tn),
        in_specs=[pl.BlockSpec((tm, K1), lambda i, j: (i, 0)),
                  pl.BlockSpec((K1, tn), lambda i, j: (0, j)),
                  pl.BlockSpec((tm, K2), lambda i, j: (i, 0)),
                  pl.BlockSpec((K2, tn), lambda i, j: (0, j))],
        out_specs=pl.BlockSpec((tm, tn), lambda i, j: (i, j)),
        out_shape=jax.ShapeDtypeStruct((M, N), F32),
        compiler_params=_cparams(("parallel", "parallel")),
    )(x1, w1, x2, w2)


def _headmm_kernel(x_ref, w_ref, o_ref, *, transpose_w):
    if transpose_w:
        o_ref[...] = _dot_nt(x_ref[...], w_ref[...]).astype(o_ref.dtype)
    else:
        o_ref[...] = _dot(x_ref[...], w_ref[...]).astype(o_ref.dtype)


def head_mm(x, w, heads, kin, kout, *, transpose_w, row_start=0, rows=None):
    M = x.shape[0]
    rows = M - row_start if rows is None else rows
    tm = _pick(rows, (1024, 512, 256, 128))
    assert row_start % tm == 0
    r0 = row_start // tm
    wblock = (kout, kin) if transpose_w else (kin, kout)
    return pl.pallas_call(
        functools.partial(_headmm_kernel, transpose_w=transpose_w),
        grid=(rows // tm, heads),
        in_specs=[pl.BlockSpec((tm, kin), lambda i, h: (i + r0, h)),
                  pl.BlockSpec(wblock, lambda i, h: (0, h))],
        out_specs=pl.BlockSpec((tm, kout), lambda i, h: (i, h)),
        out_shape=jax.ShapeDtypeStruct((rows, heads * kout), BF16),
        compiler_params=_cparams(("parallel", "parallel")),
    )(x, w)


def _ln_kernel(x_ref, y_ref, g_ref, b_ref, of_ref, ob_ref, *, alpha):
    z = alpha * x_ref[...] + y_ref[...]
    mu = jnp.mean(z, axis=-1, keepdims=True)
    zc = z - mu
    var = jnp.mean(zc * zc, axis=-1, keepdims=True)
    out = zc * lax.rsqrt(var + LN_EPS) * g_ref[...] + b_ref[...]
    of_ref[...] = out
    ob_ref[...] = out.astype(BF16)


def residual_ln(x, y, g, b, alpha):
    T, D = x.shape
    tm = _pick(T, (256, 128))
    row = pl.BlockSpec((tm, D), lambda i: (i, 0))
    vec = pl.BlockSpec((1, D), lambda i: (0, 0))
    return pl.pallas_call(
        functools.partial(_ln_kernel, alpha=alpha),
        grid=(T // tm,),
        in_specs=[row, row, vec, vec],
        out_specs=[row, row],
        out_shape=[jax.ShapeDtypeStruct((T, D), F32), jax.ShapeDtypeStruct((T, D), BF16)],
        compiler_params=_cparams(("parallel",)),
    )(x, y, g.reshape(1, D), b.reshape(1, D))


def _gla_kernel(q_ref, k_ref, v_ref, gl_ref, r_ref, wg_ref, bg_ref, gn_ref, s0_ref,
                o_ref, s_ref, state, *, chunk, n_chunks, mxu_dtype):
    rb = pl.program_id(1)

    @pl.when(rb == 0)
    def _():
        state[...] = s0_ref[0]

    C = chunk
    row = lax.broadcasted_iota(jnp.int32, (C, C), 0)
    col = lax.broadcasted_iota(jnp.int32, (C, C), 1)
    causal = row >= col
    tri = causal.astype(F32)
    ones_cv = jnp.ones((C, GLA_DV), F32)
    hi = lax.Precision.HIGHEST
    md = mxu_dtype

    for c in range(n_chunks):
        sl = slice(c * C, (c + 1) * C)
        glow = gl_ref[sl, :].astype(BF16)
        outs, states = [], []
        for h in range(GLA_HEADS):
            kcols = slice(h * GLA_DK, (h + 1) * GLA_DK)
            vcols = slice(h * GLA_DV, (h + 1) * GLA_DV)
            qi = q_ref[sl, kcols] * (GLA_DK ** -0.5)
            ki = k_ref[sl, kcols]
            vi = v_ref[sl, vcols]
            z = _dot(glow, wg_ref[:, kcols]) + bg_ref[:, kcols]
            gi = (jnp.minimum(z, 0.0) - jnp.log(1.0 + jnp.exp(-jnp.abs(z)))) * (1.0 / GLA_TAU)
            b = _dot(tri, gi, precision=hi)
            b_last = b[C - 1:C, :]
            q_g = qi * jnp.exp(b)
            k_g = ki * jnp.exp(-b)
            att = jnp.where(causal, _dot_nt(q_g.astype(md), k_g.astype(md)), 0.0)
            s_prev = state[h]
            o = _dot(att.astype(md), vi.astype(md)) + _dot(q_g.astype(md), s_prev.astype(md))
            k_dec = ki * jnp.exp(b_last - b)
            decay = _dot_tn(gi, ones_cv, precision=hi)
            states.append(jnp.exp(decay) * s_prev + _dot_tn(k_dec.astype(md), vi.astype(md)))
            ms = jnp.mean(o * o, axis=-1, keepdims=True)
            on = o * lax.rsqrt(ms + RMS_EPS) * gn_ref[...]
            rr = r_ref[sl, vcols]
            outs.append((on * (rr * jax.nn.sigmoid(rr))).astype(o_ref.dtype))
        for h in range(GLA_HEADS):
            state[h] = states[h]
            o_ref[sl, h * GLA_DV:(h + 1) * GLA_DV] = outs[h]

    @pl.when(rb == pl.num_programs(1) - 1)
    def _():
        s_ref[0] = state[...]


def gla(hm, wg, bg, gn, s0, *, row_start, batch, seq):
    C = min(GLA_CHUNK, seq)
    assert seq % C == 0
    n_chunks = _pick(seq // C, (4, 2, 1))
    R = C * n_chunks
    nrb = seq // R
    assert row_start % R == 0
    r0 = row_start // R
    H = GLA_HEADS
    QW, VW = H * GLA_DK, H * GLA_DV
    glow_blk = (2 * QW + 2 * VW + 2 * GMLP_GROUPS * GMLP_DG) // LANES
    rowmap = lambda blk: (lambda b, rb: (r0 + b * nrb + rb, blk))
    mxu_dtype = BF16 if C >= 16 else F32
    out_dtype = BF16 if R >= 16 else F32
    kernel = functools.partial(_gla_kernel, chunk=C, n_chunks=n_chunks, mxu_dtype=mxu_dtype)
    return pl.pallas_call(
        kernel,
        grid=(batch, nrb),
        in_specs=[
            pl.BlockSpec((R, QW), rowmap(0)),
            pl.BlockSpec((R, QW), rowmap(1)),
            pl.BlockSpec((R, VW), rowmap(2 * QW // VW)),
            pl.BlockSpec((R, LANES), rowmap(glow_blk)),
            pl.BlockSpec((R, VW), rowmap(2 * QW // VW + 1)),
            pl.BlockSpec((LANES, QW), lambda b, rb: (0, 0)),
            pl.BlockSpec((1, QW), lambda b, rb: (0, 0)),
            pl.BlockSpec((1, GLA_DV), lambda b, rb: (0, 0)),
            pl.BlockSpec((1, H, GLA_DK, GLA_DV), lambda b, rb: (b, 0, 0, 0)),
        ],
        out_specs=[
            pl.BlockSpec((R, VW), lambda b, rb: (b * nrb + rb, 0)),
            pl.BlockSpec((1, H, GLA_DK, GLA_DV), lambda b, rb: (b, 0, 0, 0)),
        ],
        out_shape=[jax.ShapeDtypeStruct((batch * seq, VW), out_dtype),
                   jax.ShapeDtypeStruct((batch, H, GLA_DK, GLA_DV), F32)],
        scratch_shapes=[pltpu.VMEM((H, GLA_DK, GLA_DV), F32)],
        compiler_params=_cparams(("parallel", "arbitrary")),
    )(hm, hm, hm, hm, hm, wg, bg, gn, s0)


def _gmlp_kernel(gu_ref, gv_ref, lng_ref, lnb_ref, w_ref, b_ref, m_ref, *gvo_ref):
    u = _gelu(gu_ref[...])
    v = _gelu(gv_ref[...])
    mu = jnp.mean(v, axis=-1, keepdims=True)
    vc = v - mu
    var = jnp.mean(vc * vc, axis=-1, keepdims=True)
    vn = vc * lax.rsqrt(var + LN_EPS) * lng_ref[0] + lnb_ref[0]
    s = _dot(w_ref[0], vn.astype(BF16)) + b_ref[0]
    m_ref[...] = (u * s).astype(m_ref.dtype)
    if gvo_ref:
        gvo_ref[0][...] = vn


def gmlp(hm, lng, lnb, w_eff, b_eff, *, row_start, rows, emit_v):
    G = GMLP_GROUPS
    R = GMLP_CHUNK
    assert rows % R == 0 and row_start % R == 0
    r0 = row_start // R
    ublk = (2 * GLA_HEADS * GLA_DK + 2 * GLA_HEADS * GLA_DV) // GMLP_DG
    vblk = ublk + G
    out_block = pl.BlockSpec((R, GMLP_DG), lambda i, g: (i, g))
    out_specs = [out_block]
    out_shape = [jax.ShapeDtypeStruct((rows, G * GMLP_DG), BF16)]
    if emit_v:
        out_specs.append(out_block)
        out_shape.append(jax.ShapeDtypeStruct((rows, G * GMLP_DG), F32))
    return pl.pallas_call(
        _gmlp_kernel,
        grid=(rows // R, G),
        in_specs=[
            pl.BlockSpec((R, GMLP_DG), lambda i, g: (r0 + i, ublk + g)),
            pl.BlockSpec((R, GMLP_DG), lambda i, g: (r0 + i, vblk + g)),
            pl.BlockSpec((1, 1, GMLP_DG), lambda i, g: (g, 0, 0)),
            pl.BlockSpec((1, 1, GMLP_DG), lambda i, g: (g, 0, 0)),
            pl.BlockSpec((1, R, R), lambda i, g: (g, 0, 0)),
            pl.BlockSpec((1, R, 1), lambda i, g: (g, 0, 0)),
        ],
        out_specs=out_specs,
        out_shape=out_shape,
        compiler_params=_cparams(("parallel", "parallel")),
    )(hm, hm, lng.reshape(G, 1, GMLP_DG), lnb.reshape(G, 1, GMLP_DG), w_eff, b_eff.reshape(G, R, 1))


def _mla_post_kernel(h_ref, cs_ref, qg_ref, kg_ref, cq_ref, ckv_ref, ckvb_ref, kpe_ref, kpeb_ref):
    cq = h_ref[:, :Q_LORA]
    ckv = h_ref[:, Q_LORA:Q_LORA + KV_LORA]
    cqn = cq * lax.rsqrt(jnp.mean(cq * cq, axis=-1, keepdims=True) + RMS_EPS) * qg_ref[...]
    ckvn = ckv * lax.rsqrt(jnp.mean(ckv * ckv, axis=-1, keepdims=True) + RMS_EPS) * kg_ref[...]
    cq_ref[...] = cqn.astype(BF16)
    ckv_ref[...] = ckvn
    ckvb_ref[...] = ckvn.astype(BF16)
    t = h_ref[:, Q_LORA + KV_LORA:] * cs_ref[...]
    rot = t + pltpu.roll(t, QK_ROPE, 1)
    lane = lax.broadcasted_iota(jnp.int32, rot.shape, 1)
    rot = jnp.where(lane < QK_ROPE, rot, 0.0)
    kpe_ref[...] = rot
    kpeb_ref[...] = rot.astype(BF16)


def mla_post(h, cs, q_norm_g, kv_norm_g):
    T = h.shape[0]
    tm = _pick(T, (512, 256, 128))
    W = h.shape[1]
    return pl.pallas_call(
        _mla_post_kernel,
        grid=(T // tm,),
        in_specs=[pl.BlockSpec((tm, W), lambda i: (i, 0)),
                  pl.BlockSpec((tm, LANES), lambda i: (i, 0)),
                  pl.BlockSpec((1, Q_LORA), lambda i: (0, 0)),
                  pl.BlockSpec((1, KV_LORA), lambda i: (0, 0))],
        out_specs=[pl.BlockSpec((tm, Q_LORA), lambda i: (i, 0)),
                   pl.BlockSpec((tm, KV_LORA), lambda i: (i, 0)),
                   pl.BlockSpec((tm, KV_LORA), lambda i: (i, 0)),
                   pl.BlockSpec((tm, LANES), lambda i: (i, 0)),
                   pl.BlockSpec((tm, LANES), lambda i: (i, 0))],
        out_shape=[jax.ShapeDtypeStruct((T, Q_LORA), BF16),
                   jax.ShapeDtypeStruct((T, KV_LORA), F32),
                   jax.ShapeDtypeStruct((T, KV_LORA), BF16),
                   jax.ShapeDtypeStruct((T, LANES), F32),
                   jax.ShapeDtypeStruct((T, LANES), BF16)],
        compiler_params=_cparams(("parallel",)),
    )(h, cs, q_norm_g.reshape(1, Q_LORA), kv_norm_g.reshape(1, KV_LORA))


def _qpe_kernel(cq_ref, w_ref, wsw_ref, cs_ref, o_ref):
    a = _dot(cq_ref[...], w_ref[...])
    b = _dot(cq_ref[...], wsw_ref[...])
    cos = cs_ref[:, :LANES]
    sin = cs_ref[:, LANES:]
    for h in range(MLA_HEADS):
        sl = slice(h * LANES, (h + 1) * LANES)
        o_ref[:, sl] = ((a[:, sl] * cos + b[:, sl] * sin) * MLA_SCALE).astype(o_ref.dtype)


def q_rope(cq, w_pe, w_pe_sw, cs2):
    T, K = cq.shape
    N = w_pe.shape[1]
    tm = _pick(T, (512, 256, 128))
    return pl.pallas_call(
        _qpe_kernel,
        grid=(T // tm,),
        in_specs=[pl.BlockSpec((tm, K), lambda i: (i, 0)),
                  pl.BlockSpec((K, N), lambda i: (0, 0)),
                  pl.BlockSpec((K, N), lambda i: (0, 0)),
                  pl.BlockSpec((tm, 2 * LANES), lambda i: (i, 0))],
        out_specs=pl.BlockSpec((tm, N), lambda i: (i, 0)),
        out_shape=jax.ShapeDtypeStruct((T, N), BF16),
        compiler_params=_cparams(("parallel",)),
    )(cq, w_pe, w_pe_sw, cs2)


def _flash_kernel(qn_ref, qp_ref, kn_ref, kp_ref, vt_ref, o_ref, m_scr, l_scr, acc_scr, *, tq):
    qi = pl.program_id(2)
    q = jnp.concatenate([qn_ref[...], qp_ref[...]], axis=1)
    m_scr[...] = jnp.full_like(m_scr, NEG_INF)
    l_scr[...] = jnp.zeros_like(l_scr)
    acc_scr[...] = jnp.zeros_like(acc_scr)

    def block(ki, n_blocks, masked):
        start = pl.multiple_of(ki * tq, tq)
        width = n_blocks * tq
        k = jnp.concatenate([kn_ref[pl.ds(start, width), :], kp_ref[pl.ds(start, width), :]], axis=1)
        st = _dot_nt(k, q)
        if masked:
            key = lax.broadcasted_iota(jnp.int32, st.shape, 0)
            qry = lax.broadcasted_iota(jnp.int32, st.shape, 1)
            st = jnp.where(key <= qry, st, NEG_INF)
        m_prev = m_scr[...]
        m_new = jnp.maximum(m_prev, jnp.max(st, axis=0, keepdims=True))
        alpha = jnp.exp(m_prev - m_new)
        pf = jnp.exp(st - m_new)
        p = pf.astype(BF16)
        pv = _dot(vt_ref[ki], p[:tq])
        for i in range(1, n_blocks):
            pv = pv + _dot(vt_ref[ki + i], p[i * tq:(i + 1) * tq])
        l_scr[...] = alpha * l_scr[...] + jnp.sum(pf, axis=0, keepdims=True)
        acc_scr[...] = alpha * acc_scr[...] + pv
        m_scr[...] = m_new

    def body(kp, carry):
        block(2 * kp, 2, False)
        return carry

    lax.fori_loop(0, qi // 2, body, 0)

    @pl.when(qi % 2 == 1)
    def _():
        block(qi - 1, 1, False)

    block(qi, 1, True)
    o_ref[...] = (acc_scr[...] / l_scr[...]).T.astype(o_ref.dtype)


def _vt_kernel(w_ref, c_ref, o_ref):
    o_ref[0] = _dot_nt(w_ref[...], c_ref[...]).astype(o_ref.dtype)


def value_blocks_t(ckv, w_uv_t, *, rows, tk):
    K = ckv.shape[1]
    N = w_uv_t.shape[0]
    return pl.pallas_call(
        _vt_kernel,
        grid=(rows // tk,),
        in_specs=[pl.BlockSpec((N, K), lambda i: (0, 0)),
                  pl.BlockSpec((tk, K), lambda i: (i, 0))],
        out_specs=pl.BlockSpec((1, N, tk), lambda i: (i, 0, 0)),
        out_shape=jax.ShapeDtypeStruct((rows // tk, N, tk), BF16),
        compiler_params=_cparams(("parallel",)),
    )(w_uv_t, ckv)


def mla_prompt_attention(qn, qp, kn, kp, vt, *, batch, seq, tq):
    nq = seq // tq
    H = MLA_HEADS
    return pl.pallas_call(
        functools.partial(_flash_kernel, tq=tq),
        grid=(batch, H, nq),
        in_specs=[pl.BlockSpec((tq, LANES), lambda b, h, qi: (b * nq + qi, h)),
                  pl.BlockSpec((tq, LANES), lambda b, h, qi: (b * nq + qi, h)),
                  pl.BlockSpec((seq, LANES), lambda b, h, qi: (b, h)),
                  pl.BlockSpec((seq, LANES), lambda b, h, qi: (b, 0)),
                  pl.BlockSpec((nq, V_HEAD, tq), lambda b, h, qi: (b, h, 0))],
        out_specs=pl.BlockSpec((tq, LANES), lambda b, h, qi: (b * nq + qi, h)),
        out_shape=jax.ShapeDtypeStruct((batch * seq, H * V_HEAD), BF16),
        scratch_shapes=[pltpu.VMEM((1, tq), F32), pltpu.VMEM((1, tq), F32), pltpu.VMEM((V_HEAD, tq), F32)],
        compiler_params=_cparams(("parallel", "parallel", "arbitrary")),
    )(qn, qp, kn, kp, vt)


def _paged_kernel(pt_ref, ql_ref, qp_ref, cn_ref, kn_ref, *rest, pages_per_step, dec_seq):
    P = pages_per_step
    ckv_refs = rest[:P]
    kpe_refs = rest[P:2 * P]
    o_ref = rest[2 * P]
    m_scr, l_scr, acc_scr = rest[2 * P + 1:]
    j = pl.program_id(1)
    rows = dec_seq * MLA_HEADS

    @pl.when(j == 0)
    def _():
        m_scr[...] = jnp.full_like(m_scr, NEG_INF)
        l_scr[...] = jnp.zeros_like(l_scr)
        acc_scr[...] = jnp.zeros_like(acc_scr)

    ql = ql_ref[...].reshape(rows, KV_LORA)
    qp = qp_ref[...].reshape(rows, LANES)
    qp_rope = qp[:, :QK_ROPE]

    kc = jnp.concatenate([r[0].astype(BF16) for r in ckv_refs], axis=0)
    kk_t = jnp.concatenate([r[0].astype(BF16) for r in kpe_refs], axis=1)
    s = _dot_nt(ql, kc) + _dot(qp_rope, kk_t)
    m_prev = m_scr[...]
    m_new = jnp.maximum(m_prev, jnp.max(s, axis=1, keepdims=True))
    alpha = jnp.exp(m_prev - m_new)
    p = jnp.exp(s - m_new)
    l_scr[...] = alpha * l_scr[...] + jnp.sum(p, axis=1, keepdims=True)
    acc_scr[...] = alpha * acc_scr[...] + _dot(p.astype(BF16), kc)
    m_scr[...] = m_new

    @pl.when(j == pl.num_programs(1) - 1)
    def _():
        cn = cn_ref[...]
        kn = kn_ref[...]
        sn = _dot_nt(ql.astype(F32), cn) + _dot_nt(qp.astype(F32), kn)
        t_row = lax.broadcasted_iota(jnp.int32, sn.shape, 0) // MLA_HEADS
        t_col = lax.broadcasted_iota(jnp.int32, sn.shape, 1)
        sn = jnp.where(t_col <= t_row, sn, NEG_INF)
        m_prev = m_scr[...]
        m_new = jnp.maximum(m_prev, jnp.max(sn, axis=1, keepdims=True))
        alpha = jnp.exp(m_prev - m_new)
        pn = jnp.exp(sn - m_new)
        l = alpha * l_scr[...] + jnp.sum(pn, axis=1, keepdims=True)
        acc = alpha * acc_scr[...] + _dot(pn, cn)
        o_ref[...] = (acc / l).reshape(dec_seq, MLA_HEADS, KV_LORA).astype(o_ref.dtype)


def mla_sample_attention(page_table, q_lat, qp3, ckv_f32, kpe_f32, cache_ckv, cache_kpe, page_base, *, row_start, dec_seq):
    B, n_pages = page_table.shape
    page = cache_ckv.shape[1]
    P = _pick(n_pages, (16, 8, 4, 2, 1))
    nsteps = n_pages // P
    assert row_start % dec_seq == 0
    r0 = row_start // dec_seq
    H = MLA_HEADS

    def page_map(p):
        return lambda b, j, pt: (page_base + pt[b * n_pages + j * P + p], 0, 0)

    in_specs = [
        pl.BlockSpec((dec_seq, H, KV_LORA), lambda b, j, pt: (b, 0, 0)),
        pl.BlockSpec((dec_seq, H, LANES), lambda b, j, pt: (r0 + b, 0, 0)),
        pl.BlockSpec((dec_seq, KV_LORA), lambda b, j, pt: (r0 + b, 0)),
        pl.BlockSpec((dec_seq, LANES), lambda b, j, pt: (r0 + b, 0)),
    ]
    in_specs += [pl.BlockSpec((1, page, KV_LORA), page_map(p)) for p in range(P)]
    in_specs += [pl.BlockSpec((1, QK_ROPE, page), page_map(p)) for p in range(P)]
    rows = dec_seq * H
    grid_spec = pltpu.PrefetchScalarGridSpec(
        num_scalar_prefetch=1,
        grid=(B, nsteps),
        in_specs=in_specs,
        out_specs=pl.BlockSpec((dec_seq, H, KV_LORA), lambda b, j, pt: (b, 0, 0)),
        scratch_shapes=[pltpu.VMEM((rows, 1), F32), pltpu.VMEM((rows, 1), F32), pltpu.VMEM((rows, KV_LORA), F32)],
    )
    return pl.pallas_call(
        functools.partial(_paged_kernel, pages_per_step=P, dec_seq=dec_seq),
        grid_spec=grid_spec,
        out_shape=jax.ShapeDtypeStruct((B * dec_seq, H, KV_LORA), BF16),
        compiler_params=_cparams(("parallel", "arbitrary")),
    )(page_table.reshape(-1), q_lat, qp3, ckv_f32, kpe_f32, *([cache_ckv] * P), *([cache_kpe] * P))


RANK_BASE = 1e30
RANK_STEP = 1e28


def _top_values(s, k, with_rank=False):
    out = []
    for r in range(k):
        m = jnp.max(s, axis=0, keepdims=True)
        out.append(m)
        s = jnp.where(s == m, -(RANK_BASE + r * RANK_STEP) if with_rank else -jnp.inf, s)
    if not with_rank:
        return out
    rank = jnp.where(s <= -RANK_BASE, jnp.round((-s - RANK_BASE) * (1.0 / RANK_STEP)), float(k))
    return out, rank


def _peer_topk_kernel(q_ref, keys_ref, k1_ref, c1_ref, r2_ref, e2_ref):
    K = PEER_TOPK
    for h in range(PEER_HEADS):
        qa = q_ref[:, (2 * h) * LANES:(2 * h + 1) * LANES]
        qb = q_ref[:, (2 * h + 1) * LANES:(2 * h + 2) * LANES]
        sa = _dot_nt(keys_ref[2 * h], qa)
        sb = _dot_nt(keys_ref[2 * h + 1], qb)
        ta, rank_a = _top_values(sa, K, with_rank=True)
        tb, rank_b = _top_values(sb, K, with_rank=True)
        ta_all = jnp.concatenate(ta, axis=0)
        tb_all = jnp.concatenate(tb, axis=0)
        half = K // 2
        rows_a0 = ta[0] + tb_all
        rows_lo = [ta[a] + tb_all[:half] for a in range(1, half)]
        rows_hi = ta_all[half:] + tb[0]
        best = _top_values(jnp.concatenate([rows_a0] + rows_lo + [rows_hi], axis=0), K)
        tau = best[K - 1]
        z = jnp.zeros_like(best[0])
        for r in range(K):
            z = z + jnp.exp(best[r] - best[0])
        cnt = [jnp.sum((rows >= tau).astype(F32), axis=0, keepdims=True) for rows in [rows_a0] + rows_lo]
        cnt_hi = (rows_hi >= tau).astype(F32)
        cnt += [cnt_hi[a:a + 1] for a in range(K - half)]
        k1 = jnp.zeros_like(sa)
        for a in range(K):
            k1 = jnp.where(rank_a == float(a), cnt[a], k1)
        k1_ref[h] = k1
        r2_ref[h] = rank_b.astype(r2_ref.dtype)
        c1_ref[h] = jnp.exp(sa - ta[0]) / z
        e2_ref[h] = jnp.exp(sb - tb[0]).astype(e2_ref.dtype)


def peer_topk(q, keys):
    T = q.shape[0]
    TT = _pick(T, (512, 256, 128))
    blk = pl.BlockSpec((PEER_HEADS, PEER_NKEYS, TT), lambda i: (0, 0, i))
    f32 = jax.ShapeDtypeStruct((PEER_HEADS, PEER_NKEYS, T), F32)
    b16 = jax.ShapeDtypeStruct((PEER_HEADS, PEER_NKEYS, T), BF16)
    return pl.pallas_call(
        _peer_topk_kernel,
        grid=(T // TT,),
        in_specs=[pl.BlockSpec((TT, q.shape[1]), lambda i: (i, 0)),
                  pl.BlockSpec(keys.shape, lambda i: (0, 0, 0))],
        out_specs=[blk, blk, blk, blk],
        out_shape=[f32, f32, b16, b16],
        compiler_params=_cparams(("parallel",)),
    )(q, keys)


PEER_TE = 8 * PEER_NKEYS


def _peer_dense_kernel(x_ref, u_ref, vt_ref, k1_ref, c1_ref, r2_ref, e2_ref, xf_ref, g_ref, b_ref,
                       of_ref, ob_ref, acc_scr, a_scr, *, alpha):
    j = pl.program_id(1)

    @pl.when(j == 0)
    def _():
        acc_scr[...] = jnp.zeros_like(acc_scr)

    x = x_ref[...]
    zero = jnp.zeros((), BF16)
    gps = 2
    for g in range(PEER_TE // PEER_NKEYS):
        if g % gps == 0:
            slab = slice(g * PEER_NKEYS, (g + gps) * PEER_NKEYS)
            ht = _dot_nt(u_ref[0, slab, :], x)
        gate = None
        for h in range(PEER_HEADS):
            k1 = k1_ref[h, g:g + 1, :].astype(BF16)
            c1 = c1_ref[h, g:g + 1, :].astype(BF16)
            term = jnp.where(r2_ref[h] < k1, e2_ref[h] * c1, zero)
            gate = term if gate is None else gate + term
        hs = ht[(g % gps) * PEER_NKEYS:(g % gps + 1) * PEER_NKEYS, :]
        a_scr[g * PEER_NKEYS:(g + 1) * PEER_NKEYS, :] = _gelu_sigmoid(hs.astype(BF16)) * gate
    acc_scr[...] += _dot(vt_ref[0], a_scr[...])

    @pl.when(j == pl.num_programs(1) - 1)
    def _():
        z = alpha * xf_ref[...] + acc_scr[...].T
        mu = jnp.mean(z, axis=-1, keepdims=True)
        zc = z - mu
        var = jnp.mean(zc * zc, axis=-1, keepdims=True)
        out = zc * lax.rsqrt(var + LN_EPS) * g_ref[...] + b_ref[...]
        of_ref[...] = out
        ob_ref[...] = out.astype(BF16)


def peer_dense_ln(x, xf, u, vt, layer, k1, c1, r2, e2, g, b, alpha):
    T, D = x.shape
    E = u.shape[1]
    TT = _pick(T, (512, 256, 128))
    TE = PEER_TE
    assert E % TE == 0 and E == PEER_NKEYS * PEER_NKEYS
    full = pl.BlockSpec((PEER_HEADS, PEER_NKEYS, TT), lambda i, j: (0, 0, i))
    rows8 = pl.BlockSpec((PEER_HEADS, SUBLANES, TT), lambda i, j: (0, j, i))
    row = pl.BlockSpec((TT, D), lambda i, j: (i, 0))
    row_once = pl.BlockSpec((TT, D), lambda i, j: (i, 0), pipeline_mode=pl.Buffered(1))
    vec = pl.BlockSpec((1, D), lambda i, j: (0, 0))
    return pl.pallas_call(
        functools.partial(_peer_dense_kernel, alpha=alpha),
        grid=(T // TT, E // TE),
        in_specs=[row_once,
                  pl.BlockSpec((1, TE, D), lambda i, j: (layer, j, 0)),
                  pl.BlockSpec((1, D, TE), lambda i, j: (layer, 0, j)),
                  rows8, rows8, full, full, row_once, vec, vec],
        out_specs=[row, row],
        out_shape=[jax.ShapeDtypeStruct((T, D), F32), jax.ShapeDtypeStruct((T, D), BF16)],
        scratch_shapes=[pltpu.VMEM((D, TT), F32), pltpu.VMEM((TE, TT), BF16)],
        compiler_params=_cparams(("parallel", "arbitrary")),
    )(x, u, vt, k1, c1, r2, e2, xf, g.reshape(1, D), b.reshape(1, D))


def _cast_transpose_kernel(v_ref, o_ref):
    o_ref[0] = v_ref[0].T.astype(o_ref.dtype)


def cast_transpose(v):
    L, E, D = v.shape
    te = _pick(E, (512, 256, 128))
    return pl.pallas_call(
        _cast_transpose_kernel,
        grid=(L, E // te),
        in_specs=[pl.BlockSpec((1, te, D), lambda l, j: (l, j, 0))],
        out_specs=pl.BlockSpec((1, D, te), lambda l, j: (l, 0, j)),
        out_shape=jax.ShapeDtypeStruct((L, D, E), BF16),
        compiler_params=_cparams(("parallel", "parallel")),
    )(v)


def _rope_swap_cols(w):
    half = QK_ROPE // 2
    return jnp.concatenate([-w[..., half:], w[..., :half]], axis=-1)


def _prep_even(w_in, w_gate_up, w_out):
    qk = 2 * GLA_HEADS * GLA_DK
    vw = GLA_HEADS * GLA_DV
    g0 = qk + vw
    g1 = g0 + GLA_GATE_RANK
    K = w_in.shape[0]
    w_perm = jnp.concatenate(
        [w_in[:, :g0], w_in[:, g1:], w_in[:, g0:g1], jnp.zeros((K, GLOW_PAD - GLA_GATE_RANK), w_in.dtype)], axis=1)
    wg = jnp.concatenate([w_gate_up, jnp.zeros((LANES - GLA_GATE_RANK, w_gate_up.shape[1]), w_gate_up.dtype)], axis=0)
    return w_perm.astype(BF16), wg.astype(BF16), w_out[:vw].astype(BF16), w_out[vw:].astype(BF16)


def _prep_gmlp_spatial(w_sp, b_sp, seq):
    C = min(GMLP_CHUNK, seq)
    assert seq % C == 0 and GMLP_CHUNK % C == 0
    w = jnp.where(jnp.tril(jnp.ones((C, C), bool)), w_sp[:, :C, :C], 0.0)
    b = b_sp[:, :C]
    rep = GMLP_CHUNK // C
    if rep > 1:
        eye = jnp.eye(rep, dtype=w.dtype)
        w = jnp.einsum('ab,gij->gaibj', eye, w).reshape(w.shape[0], GMLP_CHUNK, GMLP_CHUNK)
        b = jnp.tile(b, (1, rep))
    return w.astype(BF16), b


def _prep_odd(w_in, w_uq, w_uk, w_uv, w_o):
    kpe_w = w_in[:, Q_LORA + KV_LORA:]
    w_in_p = jnp.concatenate([w_in, _rope_swap_cols(kpe_w)], axis=1).astype(BF16)
    w_n = w_uq[:, :, :QK_NOPE].reshape(Q_LORA, MLA_HEADS * QK_NOPE).astype(BF16)
    pe = w_uq[:, :, QK_NOPE:]
    zpad = jnp.zeros((Q_LORA, MLA_HEADS, LANES - QK_ROPE), w_uq.dtype)
    w_pe = jnp.concatenate([pe, zpad], axis=-1).reshape(Q_LORA, MLA_HEADS * LANES).astype(BF16)
    w_pe_sw = jnp.concatenate([_rope_swap_cols(pe), zpad], axis=-1).reshape(Q_LORA, MLA_HEADS * LANES).astype(BF16)
    w_uk2 = w_uk.reshape(KV_LORA, MLA_HEADS * QK_NOPE).astype(BF16)
    w_uv2 = w_uv.reshape(KV_LORA, MLA_HEADS * V_HEAD).astype(BF16)
    return w_in_p, w_n, w_pe, w_pe_sw, w_uk2, w_uv2, w_o.astype(BF16)


def _rope_tables(pos):
    inv = ROPE_THETA ** (-jnp.arange(0, QK_ROPE, 2, dtype=F32) / QK_ROPE)
    ang = pos.astype(F32)[:, None] * inv[None, :]
    cos, sin = jnp.cos(ang), jnp.sin(ang)
    z = jnp.zeros((pos.shape[0], LANES - QK_ROPE), F32)
    cs_k = jnp.concatenate([cos, cos, sin, sin], axis=1)
    cs_q = jnp.concatenate([cos, cos, z, sin, sin, z], axis=1)
    return cs_k, cs_q


def kernel(x_prompt, x_sample, state_gla, cache_ckv, cache_kpe, page_table, w_in_even, w_gate_up, b_gate, gla_norm_g, gmlp_ln_g, gmlp_ln_b, w_spatial, b_spatial, w_out_even, w_in_odd, q_norm_g, kv_norm_g, w_uq, w_uk, w_uv, w_o_odd, ln_mix_g, ln_mix_b, ln_ffn_g, ln_ffn_b, w_peer_q, peer_keys, peer_u, peer_v):
    B, L, D = x_prompt.shape
    Bs, Ls, _ = x_sample.shape
    Tp, Ts = B * L, Bs * Ls
    T = Tp + Ts
    depth = ln_mix_g.shape[0]
    n_pool, page = cache_ckv.shape[1], cache_ckv.shape[2]
    past_len = page_table.shape[1] * page
    dn_alpha = (2.0 * depth) ** 0.25

    x = jnp.concatenate([x_prompt.reshape(Tp, D), x_sample.reshape(Ts, D)], axis=0)
    xb = x.astype(BF16)

    pos = jnp.concatenate([jnp.tile(jnp.arange(L, dtype=jnp.int32), B),
                           jnp.tile(past_len + jnp.arange(Ls, dtype=jnp.int32), Bs)])
    cs_k, cs_q = _rope_tables(pos)

    peer_u_b = peer_u.astype(BF16)
    peer_vt_b = cast_transpose(peer_v)
    cache_ckv2 = cache_ckv.reshape(-1, page, KV_LORA)
    cache_kpe2 = jnp.swapaxes(cache_kpe, 2, 3).reshape(-1, QK_ROPE, page)
    zeros_state = jnp.zeros((B, GLA_HEADS, GLA_DK, GLA_DV), F32)

    gla_p, gla_s, gmlp_s, ckv_out, kpe_out = [], [], [], [], []
    for l in range(depth):
        i = l // 2
        if l % 2 == 0:
            w_in, wg, w_out_o, w_out_m = _prep_even(w_in_even[i], w_gate_up[i], w_out_even[i])
            hm = mm(xb, w_in, F32)
            bg = b_gate[i].reshape(1, -1)
            gn = gla_norm_g[i].reshape(1, -1)
            o_p, s_p = gla(hm, wg, bg, gn, zeros_state, row_start=0, batch=B, seq=L)
            o_s, s_s = gla(hm, wg, bg, gn, state_gla[i], row_start=Tp, batch=Bs, seq=Ls)
            wsp_p, bsp_p = _prep_gmlp_spatial(w_spatial[i], b_spatial[i], L)
            wsp_s, bsp_s = _prep_gmlp_spatial(w_spatial[i], b_spatial[i], Ls)
            (m_p,) = gmlp(hm, gmlp_ln_g[i], gmlp_ln_b[i], wsp_p, bsp_p, row_start=0, rows=Tp, emit_v=False)
            m_s, gv_s = gmlp(hm, gmlp_ln_g[i], gmlp_ln_b[i], wsp_s, bsp_s, row_start=Tp, rows=Ts, emit_v=True)
            y = mm2(jnp.concatenate([o_p, o_s.astype(BF16)], axis=0), w_out_o, jnp.concatenate([m_p, m_s], axis=0), w_out_m)
            gla_p.append(s_p)
            gla_s.append(s_s)
            gmlp_s.append(gv_s.reshape(Bs, Ls, GMLP_GROUPS, GMLP_DG))
        else:
            w_in, w_n, w_pe, w_pe_sw, w_uk2, w_uv2, w_o = _prep_odd(w_in_odd[i], w_uq[i], w_uk[i], w_uv[i], w_o_odd[i])
            h = mm(xb, w_in, F32)
            cq, ckv, ckv_b, kpe, kpe_b = mla_post(h, cs_k, q_norm_g[i], kv_norm_g[i])
            qn = mm(cq, w_n, BF16, scale=MLA_SCALE)
            qp = q_rope(cq, w_pe, w_pe_sw, cs_q)
            kn = mm(ckv_b, w_uk2, BF16, rows=Tp)
            tq = _pick(L, (512, 256, 128))
            vt = value_blocks_t(ckv_b, w_uv2.T, rows=Tp, tk=tq)
            o_p = mla_prompt_attention(qn, qp, kn, kpe_b, vt, batch=B, seq=L, tq=tq)
            q_lat = head_mm(qn, w_uk2, MLA_HEADS, QK_NOPE, KV_LORA, transpose_w=True, row_start=Tp, rows=Ts)
            o_lat = mla_sample_attention(page_table, q_lat.reshape(Ts, MLA_HEADS, KV_LORA),
                                         qp.reshape(T, MLA_HEADS, LANES), ckv, kpe, cache_ckv2, cache_kpe2,
                                         i * n_pool, row_start=Tp, dec_seq=Ls)
            o_s = head_mm(o_lat.reshape(Ts, MLA_HEADS * KV_LORA), w_uv2, MLA_HEADS, KV_LORA, V_HEAD, transpose_w=False)
            y = mm(jnp.concatenate([o_p, o_s], axis=0), w_o, F32)
            ckv_out.append(ckv)
            kpe_out.append(kpe[:, :QK_ROPE])
        x, xb = residual_ln(x, y, ln_mix_g[l], ln_mix_b[l], dn_alpha)
        q = mm(xb, w_peer_q[l].astype(BF16), BF16)
        keys = peer_keys[l].reshape(PEER_HEADS * 2, PEER_NKEYS, -1).astype(BF16)
        k1, c1, r2, e2 = peer_topk(q, keys)
        x, xb = peer_dense_ln(xb, x, peer_u_b, peer_vt_b, l, k1, c1, r2, e2, ln_ffn_g[l], ln_ffn_b[l], dn_alpha)

    def split(rows_list, width):
        p = jnp.stack([r[:Tp].reshape(B, L, width) for r in rows_list])
        s = jnp.stack([r[Tp:].reshape(Bs, Ls, width) for r in rows_list])
        return p, s

    ckv_p, ckv_s = split(ckv_out, KV_LORA)
    kpe_p, kpe_s = split(kpe_out, QK_ROPE)
    return (x[:Tp].reshape(B, L, D), x[Tp:].reshape(Bs, Ls, D),
            jnp.stack(gla_p), jnp.stack(gla_s), jnp.stack(gmlp_s),
            ckv_p, kpe_p, ckv_s, kpe_s)
```

```python
import functools

import jax
import jax.numpy as jnp
from jax import lax
from jax.experimental import pallas as pl
from jax.experimental.pallas import tpu as pltpu

F32 = jnp.float32
BF16 = jnp.bfloat16

GLA_HEADS = 4
GLA_DK = 128
GLA_DV = 256
GLA_GATE_RANK = 16
GLA_TAU = 16.0
GLA_CHUNK = 64
GMLP_GROUPS = 4
GMLP_DG = 256
GMLP_CHUNK = 128
MLA_HEADS = 16
Q_LORA = 512
KV_LORA = 512
QK_NOPE = 128
QK_ROPE = 64
V_HEAD = 128
ROPE_THETA = 10000.0
MLA_SCALE = (QK_NOPE + QK_ROPE) ** -0.5
PEER_HEADS = 8
PEER_NKEYS = 128
PEER_TOPK = 16
LN_EPS = 1e-5
RMS_EPS = 1e-6
NEG_INF = -1e30

LANES = 128
SUBLANES = 8
VMEM_LIMIT_BYTES = 56 * 1024 * 1024

GLOW_PAD = 256


def _cparams(sem):
    return pltpu.CompilerParams(dimension_semantics=sem, vmem_limit_bytes=VMEM_LIMIT_BYTES)


def _pick(n, prefs):
    for p in prefs:
        if n % p == 0:
            return p
    raise ValueError(f"no tile in {prefs} divides {n}")


def _gelu(x):
    return 0.5 * x * (1.0 + jnp.tanh(0.7978845608028654 * (x + 0.044715 * (x * x * x))))


def _gelu_sigmoid(x):
    c = 2.0 * 0.7978845608028654
    neg2y = x * (x * x * (-c * 0.044715) - c)
    return x / (1.0 + jnp.exp(neg2y))


def _dot(a, b, **kw):
    return jnp.dot(a, b, preferred_element_type=F32, **kw)


def _dot_nt(a, b, **kw):
    return lax.dot_general(a, b, (((1,), (1,)), ((), ())), preferred_element_type=F32, **kw)


def _dot_tn(a, b, **kw):
    return lax.dot_general(a, b, (((0,), (0,)), ((), ())), preferred_element_type=F32, **kw)


def _mm_kernel(x_ref, w_ref, o_ref, *, scale):
    acc = _dot(x_ref[...], w_ref[...])
    if scale != 1.0:
        acc = acc * scale
    o_ref[...] = acc.astype(o_ref.dtype)


def mm(x, w, out_dtype, *, rows=None, row_start=0, scale=1.0, tn_prefs=(1792, 1024, 768, 512, 384, 256, 128)):
    M, K = x.shape
    N = w.shape[1]
    rows = M - row_start if rows is None else rows
    tm = _pick(rows, (1024, 512, 256, 128))
    assert row_start % tm == 0
    tn = _pick(N, tn_prefs)
    r0 = row_start // tm
    return pl.pallas_call(
        functools.partial(_mm_kernel, scale=scale),
        grid=(rows // tm, N // tn),
        in_specs=[pl.BlockSpec((tm, K), lambda i, j: (i + r0, 0)),
                  pl.BlockSpec((K, tn), lambda i, j: (0, j))],
        out_specs=pl.BlockSpec((tm, tn), lambda i, j: (i, j)),
        out_shape=jax.ShapeDtypeStruct((rows, N), out_dtype),
        compiler_params=_cparams(("parallel", "parallel")),
    )(x, w)


def _mm2_kernel(x1_ref, w1_ref, x2_ref, w2_ref, o_ref):
    o_ref[...] = _dot(x1_ref[...], w1_ref[...]) + _dot(x2_ref[...], w2_ref[...])


def mm2(x1, w1, x2, w2):
    M, K1 = x1.shape
    K2 = x2.shape[1]
    N = w1.shape[1]
    tm = _pick(M, (1024, 512, 256, 128))
    tn = _pick(N, (1024, 512, 256, 128))
    return pl.pallas_call(
        _mm2_kernel,
        grid=(M // tm, N // tn),
        in_specs=[pl.BlockSpec((tm, K1), lambda i, j: (i, 0)),
                  pl.BlockSpec((K1, tn), lambda i, j: (0, j)),
                  pl.BlockSpec((tm, K2), lambda i, j: (i, 0)),
                  pl.BlockSpec((K2, tn), lambda i, j: (0, j))],
        out_specs=pl.BlockSpec((tm, tn), lambda i, j: (i, j)),
        out_shape=jax.ShapeDtypeStruct((M, N), F32),
        compiler_params=_cparams(("parallel", "parallel")),
    )(x1, w1, x2, w2)


def _headmm_kernel(x_ref, w_ref, o_ref, *, transpose_w):
    if transpose_w:
        o_ref[...] = _dot_nt(x_ref[...], w_ref[...]).astype(o_ref.dtype)
    else:
        o_ref[...] = _dot(x_ref[...], w_ref[...]).astype(o_ref.dtype)


def head_mm(x, w, heads, kin, kout, *, transpose_w, row_start=0, rows=None):
    M = x.shape[0]
    rows = M - row_start if rows is None else rows
    tm = _pick(rows, (1024, 512, 256, 128))
    assert row_start % tm == 0
    r0 = row_start // tm
    wblock = (kout, kin) if transpose_w else (kin, kout)
    return pl.pallas_call(
        functools.partial(_headmm_kernel, transpose_w=transpose_w),
        grid=(rows // tm, heads),
        in_specs=[pl.BlockSpec((tm, kin), lambda i, h: (i + r0, h)),
                  pl.BlockSpec(wblock, lambda i, h: (0, h))],
        out_specs=pl.BlockSpec((tm, kout), lambda i, h: (i, h)),
        out_shape=jax.ShapeDtypeStruct((rows, heads * kout), BF16),
        compiler_params=_cparams(("parallel", "parallel")),
    )(x, w)


def _ln_kernel(x_ref, y_ref, g_ref, b_ref, of_ref, ob_ref, *, alpha):
    z = alpha * x_ref[...] + y_ref[...]
    mu = jnp.mean(z, axis=-1, keepdims=True)
    zc = z - mu
    var = jnp.mean(zc * zc, axis=-1, keepdims=True)
    out = zc * lax.rsqrt(var + LN_EPS) * g_ref[...] + b_ref[...]
    of_ref[...] = out
    ob_ref[...] = out.astype(BF16)


def residual_ln(x, y, g, b, alpha):
    T, D = x.shape
    tm = _pick(T, (256, 128))
    row = pl.BlockSpec((tm, D), lambda i: (i, 0))
    vec = pl.BlockSpec((1, D), lambda i: (0, 0))
    return pl.pallas_call(
        functools.partial(_ln_kernel, alpha=alpha),
        grid=(T // tm,),
        in_specs=[row, row, vec, vec],
        out_specs=[row, row],
        out_shape=[jax.ShapeDtypeStruct((T, D), F32), jax.ShapeDtypeStruct((T, D), BF16)],
        compiler_params=_cparams(("parallel",)),
    )(x, y, g.reshape(1, D), b.reshape(1, D))


def _gla_kernel(q_ref, k_ref, v_ref, gl_ref, r_ref, wg_ref, bg_ref, gn_ref, s0_ref,
                o_ref, s_ref, state, *, chunk, n_chunks, mxu_dtype):
    rb = pl.program_id(1)

    @pl.when(rb == 0)
    def _():
        state[...] = s0_ref[0]

    C = chunk
    row = lax.broadcasted_iota(jnp.int32, (C, C), 0)
    col = lax.broadcasted_iota(jnp.int32, (C, C), 1)
    causal = row >= col
    tri = causal.astype(F32)
    ones_cv = jnp.ones((C, GLA_DV), F32)
    hi = lax.Precision.HIGHEST
    md = mxu_dtype

    for c in range(n_chunks):
        sl = slice(c * C, (c + 1) * C)
        glow = gl_ref[sl, :].astype(BF16)
        outs, states = [], []
        for h in range(GLA_HEADS):
            kcols = slice(h * GLA_DK, (h + 1) * GLA_DK)
            vcols = slice(h * GLA_DV, (h + 1) * GLA_DV)
            qi = q_ref[sl, kcols] * (GLA_DK ** -0.5)
            ki = k_ref[sl, kcols]
            vi = v_ref[sl, vcols]
            z = _dot(glow, wg_ref[:, kcols]) + bg_ref[:, kcols]
            gi = (jnp.minimum(z, 0.0) - jnp.log(1.0 + jnp.exp(-jnp.abs(z)))) * (1.0 / GLA_TAU)
            b = _dot(tri, gi, precision=hi)
            b_last = b[C - 1:C, :]
            q_g = qi * jnp.exp(b)
            k_g = ki * jnp.exp(-b)
            att = jnp.where(causal, _dot_nt(q_g.astype(md), k_g.astype(md)), 0.0)
            s_prev = state[h]
            o = _dot(att.astype(md), vi.astype(md)) + _dot(q_g.astype(md), s_prev.astype(md))
            k_dec = ki * jnp.exp(b_last - b)
            decay = _dot_tn(gi, ones_cv, precision=hi)
            states.append(jnp.exp(decay) * s_prev + _dot_tn(k_dec.astype(md), vi.astype(md)))
            ms = jnp.mean(o * o, axis=-1, keepdims=True)
            on = o * lax.rsqrt(ms + RMS_EPS) * gn_ref[...]
            rr = r_ref[sl, vcols]
            outs.append((on * (rr * jax.nn.sigmoid(rr))).astype(o_ref.dtype))
        for h in range(GLA_HEADS):
            state[h] = states[h]
            o_ref[sl, h * GLA_DV:(h + 1) * GLA_DV] = outs[h]

    @pl.when(rb == pl.num_programs(1) - 1)
    def _():
        s_ref[0] = state[...]


def gla(hm, wg, bg, gn, s0, *, row_start, batch, seq):
    C = min(GLA_CHUNK, seq)
    assert seq % C == 0
    n_chunks = _pick(seq // C, (4, 2, 1))
    R = C * n_chunks
    nrb = seq // R
    assert row_start % R == 0
    r0 = row_start // R
    H = GLA_HEADS
    QW, VW = H * GLA_DK, H * GLA_DV
    glow_blk = (2 * QW + 2 * VW + 2 * GMLP_GROUPS * GMLP_DG) // LANES
    rowmap = lambda blk: (lambda b, rb: (r0 + b * nrb + rb, blk))
    mxu_dtype = BF16 if C >= 16 else F32
    out_dtype = BF16 if R >= 16 else F32
    kernel = functools.partial(_gla_kernel, chunk=C, n_chunks=n_chunks, mxu_dtype=mxu_dtype)
    return pl.pallas_call(
        kernel,
        grid=(batch, nrb),
        in_specs=[
            pl.BlockSpec((R, QW), rowmap(0)),
            pl.BlockSpec((R, QW), rowmap(1)),
            pl.BlockSpec((R, VW), rowmap(2 * QW // VW)),
            pl.BlockSpec((R, LANES), rowmap(glow_blk)),
            pl.BlockSpec((R, VW), rowmap(2 * QW // VW + 1)),
            pl.BlockSpec((LANES, QW), lambda b, rb: (0, 0)),
            pl.BlockSpec((1, QW), lambda b, rb: (0, 0)),
            pl.BlockSpec((1, GLA_DV), lambda b, rb: (0, 0)),
            pl.BlockSpec((1, H, GLA_DK, GLA_DV), lambda b, rb: (b, 0, 0, 0)),
        ],
        out_specs=[
            pl.BlockSpec((R, VW), lambda b, rb: (b * nrb + rb, 0)),
            pl.BlockSpec((1, H, GLA_DK, GLA_DV), lambda b, rb: (b, 0, 0, 0)),
        ],
        out_shape=[jax.ShapeDtypeStruct((batch * seq, VW), out_dtype),
                   jax.ShapeDtypeStruct((batch, H, GLA_DK, GLA_DV), F32)],
        scratch_shapes=[pltpu.VMEM((H, GLA_DK, GLA_DV), F32)],
        compiler_params=_cparams(("parallel", "arbitrary")),
    )(hm, hm, hm, hm, hm, wg, bg, gn, s0)


def _gmlp_kernel(gu_ref, gv_ref, lng_ref, lnb_ref, w_ref, b_ref, m_ref, *gvo_ref):
    u = _gelu(gu_ref[...])
    v = _gelu(gv_ref[...])
    mu = jnp.mean(v, axis=-1, keepdims=True)
    vc = v - mu
    var = jnp.mean(vc * vc, axis=-1, keepdims=True)
    vn = vc * lax.rsqrt(var + LN_EPS) * lng_ref[0] + lnb_ref[0]
    s = _dot(w_ref[0], vn.astype(BF16)) + b_ref[0]
    m_ref[...] = (u * s).astype(m_ref.dtype)
    if gvo_ref:
        gvo_ref[0][...] = vn


def gmlp(hm, lng, lnb, w_eff, b_eff, *, row_start, rows, emit_v):
    G = GMLP_GROUPS
    R = GMLP_CHUNK
    assert rows % R == 0 and row_start % R == 0
    r0 = row_start // R
    ublk = (2 * GLA_HEADS * GLA_DK + 2 * GLA_HEADS * GLA_DV) // GMLP_DG
    vblk = ublk + G
    out_block = pl.BlockSpec((R, GMLP_DG), lambda i, g: (i, g))
    out_specs = [out_block]
    out_shape = [jax.ShapeDtypeStruct((rows, G * GMLP_DG), BF16)]
    if emit_v:
        out_specs.append(out_block)
        out_shape.append(jax.ShapeDtypeStruct((rows, G * GMLP_DG), F32))
    return pl.pallas_call(
        _gmlp_kernel,
        grid=(rows // R, G),
        in_specs=[
            pl.BlockSpec((R, GMLP_DG), lambda i, g: (r0 + i, ublk + g)),
            pl.BlockSpec((R, GMLP_DG), lambda i, g: (r0 + i, vblk + g)),
            pl.BlockSpec((1, 1, GMLP_DG), lambda i, g: (g, 0, 0)),
            pl.BlockSpec((1, 1, GMLP_DG), lambda i, g: (g, 0, 0)),
            pl.BlockSpec((1, R, R), lambda i, g: (g, 0, 0)),
            pl.BlockSpec((1, R, 1), lambda i, g: (g, 0, 0)),
        ],
        out_specs=out_specs,
        out_shape=out_shape,
        compiler_params=_cparams(("parallel", "parallel")),
    )(hm, hm, lng.reshape(G, 1, GMLP_DG), lnb.reshape(G, 1, GMLP_DG), w_eff, b_eff.reshape(G, R, 1))


def _mla_post_kernel(h_ref, cs_ref, qg_ref, kg_ref, cq_ref, ckv_ref, ckvb_ref, kpe_ref, kpeb_ref):
    cq = h_ref[:, :Q_LORA]
    ckv = h_ref[:, Q_LORA:Q_LORA + KV_LORA]
    cqn = cq * lax.rsqrt(jnp.mean(cq * cq, axis=-1, keepdims=True) + RMS_EPS) * qg_ref[...]
    ckvn = ckv * lax.rsqrt(jnp.mean(ckv * ckv, axis=-1, keepdims=True) + RMS_EPS) * kg_ref[...]
    cq_ref[...] = cqn.astype(BF16)
    ckv_ref[...] = ckvn
    ckvb_ref[...] = ckvn.astype(BF16)
    t = h_ref[:, Q_LORA + KV_LORA:] * cs_ref[...]
    rot = t + pltpu.roll(t, QK_ROPE, 1)
    lane = lax.broadcasted_iota(jnp.int32, rot.shape, 1)
    rot = jnp.where(lane < QK_ROPE, rot, 0.0)
    kpe_ref[...] = rot
    kpeb_ref[...] = rot.astype(BF16)


def mla_post(h, cs, q_norm_g, kv_norm_g):
    T = h.shape[0]
    tm = _pick(T, (512, 256, 128))
    W = h.shape[1]
    return pl.pallas_call(
        _mla_post_kernel,
        grid=(T // tm,),
        in_specs=[pl.BlockSpec((tm, W), lambda i: (i, 0)),
                  pl.BlockSpec((tm, LANES), lambda i: (i, 0)),
                  pl.BlockSpec((1, Q_LORA), lambda i: (0, 0)),
                  pl.BlockSpec((1, KV_LORA), lambda i: (0, 0))],
        out_specs=[pl.BlockSpec((tm, Q_LORA), lambda i: (i, 0)),
                   pl.BlockSpec((tm, KV_LORA), lambda i: (i, 0)),
                   pl.BlockSpec((tm, KV_LORA), lambda i: (i, 0)),
                   pl.BlockSpec((tm, LANES), lambda i: (i, 0)),
                   pl.BlockSpec((tm, LANES), lambda i: (i, 0))],
        out_shape=[jax.ShapeDtypeStruct((T, Q_LORA), BF16),
                   jax.ShapeDtypeStruct((T, KV_LORA), F32),
                   jax.ShapeDtypeStruct((T, KV_LORA), BF16),
                   jax.ShapeDtypeStruct((T, LANES), F32),
                   jax.ShapeDtypeStruct((T, LANES), BF16)],
        compiler_params=_cparams(("parallel",)),
    )(h, cs, q_norm_g.reshape(1, Q_LORA), kv_norm_g.reshape(1, KV_LORA))


def _qpe_kernel(cq_ref, w_ref, wsw_ref, cs_ref, o_ref):
    a = _dot(cq_ref[...], w_ref[...])
    b = _dot(cq_ref[...], wsw_ref[...])
    cos = cs_ref[:, :LANES]
    sin = cs_ref[:, LANES:]
    for h in range(MLA_HEADS):
        sl = slice(h * LANES, (h + 1) * LANES)
        o_ref[:, sl] = ((a[:, sl] * cos + b[:, sl] * sin) * MLA_SCALE).astype(o_ref.dtype)


def q_rope(cq, w_pe, w_pe_sw, cs2):
    T, K = cq.shape
    N = w_pe.shape[1]
    tm = _pick(T, (512, 256, 128))
    return pl.pallas_call(
        _qpe_kernel,
        grid=(T // tm,),
        in_specs=[pl.BlockSpec((tm, K), lambda i: (i, 0)),
                  pl.BlockSpec((K, N), lambda i: (0, 0)),
                  pl.BlockSpec((K, N), lambda i: (0, 0)),
                  pl.BlockSpec((tm, 2 * LANES), lambda i: (i, 0))],
        out_specs=pl.BlockSpec((tm, N), lambda i: (i, 0)),
        out_shape=jax.ShapeDtypeStruct((T, N), BF16),
        compiler_params=_cparams(("parallel",)),
    )(cq, w_pe, w_pe_sw, cs2)


def _flash_kernel(qn_ref, qp_ref, kn_ref, kp_ref, vt_ref, o_ref, m_scr, l_scr, acc_scr, *, tq):
    qi = pl.program_id(2)
    q = jnp.concatenate([qn_ref[...], qp_ref[...]], axis=1)
    m_scr[...] = jnp.full_like(m_scr, NEG_INF)
    l_scr[...] = jnp.zeros_like(l_scr)
    acc_scr[...] = jnp.zeros_like(acc_scr)

    def block(ki, n_blocks, masked):
        start = pl.multiple_of(ki * tq, tq)
        width = n_blocks * tq
        k = jnp.concatenate([kn_ref[pl.ds(start, width), :], kp_ref[pl.ds(start, width), :]], axis=1)
        st = _dot_nt(k, q)
        if masked:
            key = lax.broadcasted_iota(jnp.int32, st.shape, 0)
            qry = lax.broadcasted_iota(jnp.int32, st.shape, 1)
            st = jnp.where(key <= qry, st, NEG_INF)
        m_prev = m_scr[...]
        m_new = jnp.maximum(m_prev, jnp.max(st, axis=0, keepdims=True))
        alpha = jnp.exp(m_prev - m_new)
        pf = jnp.exp(st - m_new)
        p = pf.astype(BF16)
        pv = _dot(vt_ref[ki], p[:tq])
        for i in range(1, n_blocks):
            pv = pv + _dot(vt_ref[ki + i], p[i * tq:(i + 1) * tq])
        l_scr[...] = alpha * l_scr[...] + jnp.sum(pf, axis=0, keepdims=True)
        acc_scr[...] = alpha * acc_scr[...] + pv
        m_scr[...] = m_new

    def body(kp, carry):
        block(2 * kp, 2, False)
        return carry

    lax.fori_loop(0, qi // 2, body, 0)

    @pl.when(qi % 2 == 1)
    def _():
        block(qi - 1, 1, False)

    block(qi, 1, True)
    o_ref[...] = (acc_scr[...] / l_scr[...]).T.astype(o_ref.dtype)


def _vt_kernel(w_ref, c_ref, o_ref):
    o_ref[0] = _dot_nt(w_ref[...], c_ref[...]).astype(o_ref.dtype)


def value_blocks_t(ckv, w_uv_t, *, rows, tk):
    K = ckv.shape[1]
    N = w_uv_t.shape[0]
    return pl.pallas_call(
        _vt_kernel,
        grid=(rows // tk,),
        in_specs=[pl.BlockSpec((N, K), lambda i: (0, 0)),
                  pl.BlockSpec((tk, K), lambda i: (i, 0))],
        out_specs=pl.BlockSpec((1, N, tk), lambda i: (i, 0, 0)),
        out_shape=jax.ShapeDtypeStruct((rows // tk, N, tk), BF16),
        compiler_params=_cparams(("parallel",)),
    )(w_uv_t, ckv)


def mla_prompt_attention(qn, qp, kn, kp, vt, *, batch, seq, tq):
    nq = seq // tq
    H = MLA_HEADS
    return pl.pallas_call(
        functools.partial(_flash_kernel, tq=tq),
        grid=(batch, H, nq),
        in_specs=[pl.BlockSpec((tq, LANES), lambda b, h, qi: (b * nq + qi, h)),
                  pl.BlockSpec((tq, LANES), lambda b, h, qi: (b * nq + qi, h)),
                  pl.BlockSpec((seq, LANES), lambda b, h, qi: (b, h)),
                  pl.BlockSpec((seq, LANES), lambda b, h, qi: (b, 0)),
                  pl.BlockSpec((nq, V_HEAD, tq), lambda b, h, qi: (b, h, 0))],
        out_specs=pl.BlockSpec((tq, LANES), lambda b, h, qi: (b * nq + qi, h)),
        out_shape=jax.ShapeDtypeStruct((batch * seq, H * V_HEAD), BF16),
        scratch_shapes=[pltpu.VMEM((1, tq), F32), pltpu.VMEM((1, tq), F32), pltpu.VMEM((V_HEAD, tq), F32)],
        compiler_params=_cparams(("parallel", "parallel", "arbitrary")),
    )(qn, qp, kn, kp, vt)


def _paged_kernel(pt_ref, ql_ref, qp_ref, cn_ref, kn_ref, *rest, pages_per_step, dec_seq):
    P = pages_per_step
    ckv_refs = rest[:P]
    kpe_refs = rest[P:2 * P]
    o_ref = rest[2 * P]
    m_scr, l_scr, acc_scr = rest[2 * P + 1:]
    j = pl.program_id(1)
    rows = dec_seq * MLA_HEADS

    @pl.when(j == 0)
    def _():
        m_scr[...] = jnp.full_like(m_scr, NEG_INF)
        l_scr[...] = jnp.zeros_like(l_scr)
        acc_scr[...] = jnp.zeros_like(acc_scr)

    ql = ql_ref[...].reshape(rows, KV_LORA)
    qp = qp_ref[...].reshape(rows, LANES)
    qp_rope = qp[:, :QK_ROPE]

    kc = jnp.concatenate([r[0].astype(BF16) for r in ckv_refs], axis=0)
    kk_t = jnp.concatenate([r[0].astype(BF16) for r in kpe_refs], axis=1)
    s = _dot_nt(ql, kc) + _dot(qp_rope, kk_t)
    m_prev = m_scr[...]
    m_new = jnp.maximum(m_prev, jnp.max(s, axis=1, keepdims=True))
    alpha = jnp.exp(m_prev - m_new)
    p = jnp.exp(s - m_new)
    l_scr[...] = alpha * l_scr[...] + jnp.sum(p, axis=1, keepdims=True)
    acc_scr[...] = alpha * acc_scr[...] + _dot(p.astype(BF16), kc)
    m_scr[...] = m_new

    @pl.when(j == pl.num_programs(1) - 1)
    def _():
        cn = cn_ref[...]
        kn = kn_ref[...]
        sn = _dot_nt(ql.astype(F32), cn) + _dot_nt(qp.astype(F32), kn)
        t_row = lax.broadcasted_iota(jnp.int32, sn.shape, 0) // MLA_HEADS
        t_col = lax.broadcasted_iota(jnp.int32, sn.shape, 1)
        sn = jnp.where(t_col <= t_row, sn, NEG_INF)
        m_prev = m_scr[...]
        m_new = jnp.maximum(m_prev, jnp.max(sn, axis=1, keepdims=True))
        alpha = jnp.exp(m_prev - m_new)
        pn = jnp.exp(sn - m_new)
        l = alpha * l_scr[...] + jnp.sum(pn, axis=1, keepdims=True)
        acc = alpha * acc_scr[...] + _dot(pn, cn)
        o_ref[...] = (acc / l).reshape(dec_seq, MLA_HEADS, KV_LORA).astype(o_ref.dtype)


def mla_sample_attention(page_table, q_lat, qp3, ckv_f32, kpe_f32, cache_ckv, cache_kpe, page_base, *, row_start, dec_seq):
    B, n_pages = page_table.shape
    page = cache_ckv.shape[1]
    P = _pick(n_pages, (32, 16, 8, 4, 2, 1))
    nsteps = n_pages // P
    assert row_start % dec_seq == 0
    r0 = row_start // dec_seq
    H = MLA_HEADS

    def page_map(p):
        return lambda b, j, pt: (page_base + pt[b * n_pages + j * P + p], 0, 0)

    in_specs = [
        pl.BlockSpec((dec_seq, H, KV_LORA), lambda b, j, pt: (b, 0, 0)),
        pl.BlockSpec((dec_seq, H, LANES), lambda b, j, pt: (r0 + b, 0, 0)),
        pl.BlockSpec((dec_seq, KV_LORA), lambda b, j, pt: (r0 + b, 0)),
        pl.BlockSpec((dec_seq, LANES), lambda b, j, pt: (r0 + b, 0)),
    ]
    in_specs += [pl.BlockSpec((1, page, KV_LORA), page_map(p)) for p in range(P)]
    in_specs += [pl.BlockSpec((1, QK_ROPE, page), page_map(p)) for p in range(P)]
    rows = dec_seq * H
    grid_spec = pltpu.PrefetchScalarGridSpec(
        num_scalar_prefetch=1,
        grid=(B, nsteps),
        in_specs=in_specs,
        out_specs=pl.BlockSpec((dec_seq, H, KV_LORA), lambda b, j, pt: (b, 0, 0)),
        scratch_shapes=[pltpu.VMEM((rows, 1), F32), pltpu.VMEM((rows, 1), F32), pltpu.VMEM((rows, KV_LORA), F32)],
    )
    return pl.pallas_call(
        functools.partial(_paged_kernel, pages_per_step=P, dec_seq=dec_seq),
        grid_spec=grid_spec,
        out_shape=jax.ShapeDtypeStruct((B * dec_seq, H, KV_LORA), BF16),
        compiler_params=_cparams(("parallel", "arbitrary")),
    )(page_table.reshape(-1), q_lat, qp3, ckv_f32, kpe_f32, *([cache_ckv] * P), *([cache_kpe] * P))


RANK_BASE = 1e30
RANK_STEP = 1e28


def _top_values(s, k, with_rank=False):
    out = []
    for r in range(k):
        m = jnp.max(s, axis=0, keepdims=True)
        out.append(m)
        s = jnp.where(s == m, -(RANK_BASE + r * RANK_STEP) if with_rank else -jnp.inf, s)
    if not with_rank:
        return out
    rank = jnp.where(s <= -RANK_BASE, jnp.round((-s - RANK_BASE) * (1.0 / RANK_STEP)), float(k))
    return out, rank


def _peer_topk_kernel(q_ref, keys_ref, k1_ref, c1_ref, r2_ref, e2_ref):
    K = PEER_TOPK
    for h in range(PEER_HEADS):
        qa = q_ref[:, (2 * h) * LANES:(2 * h + 1) * LANES]
        qb = q_ref[:, (2 * h + 1) * LANES:(2 * h + 2) * LANES]
        sa = _dot_nt(keys_ref[2 * h], qa)
        sb = _dot_nt(keys_ref[2 * h + 1], qb)
        ta, rank_a = _top_values(sa, K, with_rank=True)
        tb, rank_b = _top_values(sb, K, with_rank=True)
        ta_all = jnp.concatenate(ta, axis=0)
        tb_all = jnp.concatenate(tb, axis=0)
        half = K // 2
        rows_a0 = ta[0] + tb_all
        rows_lo = [ta[a] + tb_all[:half] for a in range(1, half)]
        rows_hi = ta_all[half:] + tb[0]
        best = _top_values(jnp.concatenate([rows_a0] + rows_lo + [rows_hi], axis=0), K)
        tau = best[K - 1]
        z = jnp.zeros_like(best[0])
        for r in range(K):
            z = z + jnp.exp(best[r] - best[0])
        cnt = [jnp.sum((rows >= tau).astype(F32), axis=0, keepdims=True) for rows in [rows_a0] + rows_lo]
        cnt_hi = (rows_hi >= tau).astype(F32)
        cnt += [cnt_hi[a:a + 1] for a in range(K - half)]
        k1 = jnp.zeros_like(sa)
        for a in range(K):
            k1 = jnp.where(rank_a == float(a), cnt[a], k1)
        k1_ref[h] = k1
        r2_ref[h] = rank_b.astype(r2_ref.dtype)
        c1_ref[h] = jnp.exp(sa - ta[0]) / z
        e2_ref[h] = jnp.exp(sb - tb[0]).astype(e2_ref.dtype)


def peer_topk(q, keys):
    T = q.shape[0]
    TT = _pick(T, (512, 256, 128))
    blk = pl.BlockSpec((PEER_HEADS, PEER_NKEYS, TT), lambda i: (0, 0, i))
    f32 = jax.ShapeDtypeStruct((PEER_HEADS, PEER_NKEYS, T), F32)
    b16 = jax.ShapeDtypeStruct((PEER_HEADS, PEER_NKEYS, T), BF16)
    return pl.pallas_call(
        _peer_topk_kernel,
        grid=(T // TT,),
        in_specs=[pl.BlockSpec((TT, q.shape[1]), lambda i: (i, 0)),
                  pl.BlockSpec(keys.shape, lambda i: (0, 0, 0))],
        out_specs=[blk, blk, blk, blk],
        out_shape=[f32, f32, b16, b16],
        compiler_params=_cparams(("parallel",)),
    )(q, keys)


PEER_TE = 8 * PEER_NKEYS


def _peer_dense_kernel(x_ref, u_ref, vt_ref, k1_ref, c1_ref, r2_ref, e2_ref, o_ref, acc_scr, a_scr):
    j = pl.program_id(1)

    @pl.when(j == 0)
    def _():
        acc_scr[...] = jnp.zeros_like(acc_scr)

    x = x_ref[...]
    zero = jnp.zeros((), BF16)
    gps = 2
    for g in range(PEER_TE // PEER_NKEYS):
        if g % gps == 0:
            slab = slice(g * PEER_NKEYS, (g + gps) * PEER_NKEYS)
            ht = _dot_nt(u_ref[0, slab, :], x)
        gate = None
        for h in range(PEER_HEADS):
            k1 = k1_ref[h, g:g + 1, :].astype(BF16)
            c1 = c1_ref[h, g:g + 1, :].astype(BF16)
            term = jnp.where(r2_ref[h] < k1, e2_ref[h] * c1, zero)
            gate = term if gate is None else gate + term
        hs = ht[(g % gps) * PEER_NKEYS:(g % gps + 1) * PEER_NKEYS, :]
        a_scr[g * PEER_NKEYS:(g + 1) * PEER_NKEYS, :] = _gelu_sigmoid(hs.astype(BF16)) * gate
    acc_scr[...] += _dot(vt_ref[0], a_scr[...])

    @pl.when(j == pl.num_programs(1) - 1)
    def _():
        o_ref[...] = acc_scr[...].T


def peer_dense(x, u, vt, layer, k1, c1, r2, e2):
    T, D = x.shape
    E = u.shape[1]
    TT = _pick(T, (512, 256, 128))
    TE = PEER_TE
    assert E % TE == 0 and E == PEER_NKEYS * PEER_NKEYS
    full = pl.BlockSpec((PEER_HEADS, PEER_NKEYS, TT), lambda i, j: (0, 0, i))
    rows8 = pl.BlockSpec((PEER_HEADS, SUBLANES, TT), lambda i, j: (0, j, i))
    return pl.pallas_call(
        _peer_dense_kernel,
        grid=(T // TT, E // TE),
        in_specs=[pl.BlockSpec((TT, D), lambda i, j: (i, 0)),
                  pl.BlockSpec((1, TE, D), lambda i, j: (layer, j, 0)),
                  pl.BlockSpec((1, D, TE), lambda i, j: (layer, 0, j)),
                  rows8, rows8, full, full],
        out_specs=pl.BlockSpec((TT, D), lambda i, j: (i, 0)),
        out_shape=jax.ShapeDtypeStruct((T, D), F32),
        scratch_shapes=[pltpu.VMEM((D, TT), F32), pltpu.VMEM((TE, TT), BF16)],
        compiler_params=_cparams(("parallel", "arbitrary")),
    )(x, u, vt, k1, c1, r2, e2)


def _cast_transpose_kernel(v_ref, o_ref):
    o_ref[0] = v_ref[0].T.astype(o_ref.dtype)


def cast_transpose(v):
    L, E, D = v.shape
    te = _pick(E, (512, 256, 128))
    return pl.pallas_call(
        _cast_transpose_kernel,
        grid=(L, E // te),
        in_specs=[pl.BlockSpec((1, te, D), lambda l, j: (l, j, 0))],
        out_specs=pl.BlockSpec((1, D, te), lambda l, j: (l, 0, j)),
        out_shape=jax.ShapeDtypeStruct((L, D, E), BF16),
        compiler_params=_cparams(("parallel", "parallel")),
    )(v)


def _rope_swap_cols(w):
    half = QK_ROPE // 2
    return jnp.concatenate([-w[..., half:], w[..., :half]], axis=-1)


def _prep_even(w_in, w_gate_up, w_out):
    qk = 2 * GLA_HEADS * GLA_DK
    vw = GLA_HEADS * GLA_DV
    g0 = qk + vw
    g1 = g0 + GLA_GATE_RANK
    K = w_in.shape[0]
    w_perm = jnp.concatenate(
        [w_in[:, :g0], w_in[:, g1:], w_in[:, g0:g1], jnp.zeros((K, GLOW_PAD - GLA_GATE_RANK), w_in.dtype)], axis=1)
    wg = jnp.concatenate([w_gate_up, jnp.zeros((LANES - GLA_GATE_RANK, w_gate_up.shape[1]), w_gate_up.dtype)], axis=0)
    return w_perm.astype(BF16), wg.astype(BF16), w_out[:vw].astype(BF16), w_out[vw:].astype(BF16)


def _prep_gmlp_spatial(w_sp, b_sp, seq):
    C = min(GMLP_CHUNK, seq)
    assert seq % C == 0 and GMLP_CHUNK % C == 0
    w = jnp.where(jnp.tril(jnp.ones((C, C), bool)), w_sp[:, :C, :C], 0.0)
    b = b_sp[:, :C]
    rep = GMLP_CHUNK // C
    if rep > 1:
        eye = jnp.eye(rep, dtype=w.dtype)
        w = jnp.einsum('ab,gij->gaibj', eye, w).reshape(w.shape[0], GMLP_CHUNK, GMLP_CHUNK)
        b = jnp.tile(b, (1, rep))
    return w.astype(BF16), b


def _prep_odd(w_in, w_uq, w_uk, w_uv, w_o):
    kpe_w = w_in[:, Q_LORA + KV_LORA:]
    w_in_p = jnp.concatenate([w_in, _rope_swap_cols(kpe_w)], axis=1).astype(BF16)
    w_n = w_uq[:, :, :QK_NOPE].reshape(Q_LORA, MLA_HEADS * QK_NOPE).astype(BF16)
    pe = w_uq[:, :, QK_NOPE:]
    zpad = jnp.zeros((Q_LORA, MLA_HEADS, LANES - QK_ROPE), w_uq.dtype)
    w_pe = jnp.concatenate([pe, zpad], axis=-1).reshape(Q_LORA, MLA_HEADS * LANES).astype(BF16)
    w_pe_sw = jnp.concatenate([_rope_swap_cols(pe), zpad], axis=-1).reshape(Q_LORA, MLA_HEADS * LANES).astype(BF16)
    w_uk2 = w_uk.reshape(KV_LORA, MLA_HEADS * QK_NOPE).astype(BF16)
    w_uv2 = w_uv.reshape(KV_LORA, MLA_HEADS * V_HEAD).astype(BF16)
    return w_in_p, w_n, w_pe, w_pe_sw, w_uk2, w_uv2, w_o.astype(BF16)


def _rope_tables(pos):
    inv = ROPE_THETA ** (-jnp.arange(0, QK_ROPE, 2, dtype=F32) / QK_ROPE)
    ang = pos.astype(F32)[:, None] * inv[None, :]
    cos, sin = jnp.cos(ang), jnp.sin(ang)
    z = jnp.zeros((pos.shape[0], LANES - QK_ROPE), F32)
    cs_k = jnp.concatenate([cos, cos, sin, sin], axis=1)
    cs_q = jnp.concatenate([cos, cos, z, sin, sin, z], axis=1)
    return cs_k, cs_q


def kernel(x_prompt, x_sample, state_gla, cache_ckv, cache_kpe, page_table, w_in_even, w_gate_up, b_gate, gla_norm_g, gmlp_ln_g, gmlp_ln_b, w_spatial, b_spatial, w_out_even, w_in_odd, q_norm_g, kv_norm_g, w_uq, w_uk, w_uv, w_o_odd, ln_mix_g, ln_mix_b, ln_ffn_g, ln_ffn_b, w_peer_q, peer_keys, peer_u, peer_v):
    B, L, D = x_prompt.shape
    Bs, Ls, _ = x_sample.shape
    Tp, Ts = B * L, Bs * Ls
    T = Tp + Ts
    depth = ln_mix_g.shape[0]
    n_pool, page = cache_ckv.shape[1], cache_ckv.shape[2]
    past_len = page_table.shape[1] * page
    dn_alpha = (2.0 * depth) ** 0.25

    x = jnp.concatenate([x_prompt.reshape(Tp, D), x_sample.reshape(Ts, D)], axis=0)
    xb = x.astype(BF16)

    pos = jnp.concatenate([jnp.tile(jnp.arange(L, dtype=jnp.int32), B),
                           jnp.tile(past_len + jnp.arange(Ls, dtype=jnp.int32), Bs)])
    cs_k, cs_q = _rope_tables(pos)

    peer_u_b = peer_u.astype(BF16)
    peer_vt_b = cast_transpose(peer_v)
    cache_ckv2 = cache_ckv.reshape(-1, page, KV_LORA)
    cache_kpe2 = jnp.swapaxes(cache_kpe, 2, 3).reshape(-1, QK_ROPE, page)
    zeros_state = jnp.zeros((B, GLA_HEADS, GLA_DK, GLA_DV), F32)

    gla_p, gla_s, gmlp_s, ckv_out, kpe_out = [], [], [], [], []
    for l in range(depth):
        i = l // 2
        if l % 2 == 0:
            w_in, wg, w_out_o, w_out_m = _prep_even(w_in_even[i], w_gate_up[i], w_out_even[i])
            hm = mm(xb, w_in, F32)
            bg = b_gate[i].reshape(1, -1)
            gn = gla_norm_g[i].reshape(1, -1)
            o_p, s_p = gla(hm, wg, bg, gn, zeros_state, row_start=0, batch=B, seq=L)
            o_s, s_s = gla(hm, wg, bg, gn, state_gla[i], row_start=Tp, batch=Bs, seq=Ls)
            wsp_p, bsp_p = _prep_gmlp_spatial(w_spatial[i], b_spatial[i], L)
            wsp_s, bsp_s = _prep_gmlp_spatial(w_spatial[i], b_spatial[i], Ls)
            (m_p,) = gmlp(hm, gmlp_ln_g[i], gmlp_ln_b[i], wsp_p, bsp_p, row_start=0, rows=Tp, emit_v=False)
            m_s, gv_s = gmlp(hm, gmlp_ln_g[i], gmlp_ln_b[i], wsp_s, bsp_s, row_start=Tp, rows=Ts, emit_v=True)
            y = mm2(jnp.concatenate([o_p, o_s.astype(BF16)], axis=0), w_out_o, jnp.concatenate([m_p, m_s], axis=0), w_out_m)
            gla_p.append(s_p)
            gla_s.append(s_s)
            gmlp_s.append(gv_s.reshape(Bs, Ls, GMLP_GROUPS, GMLP_DG))
        else:
            w_in, w_n, w_pe, w_pe_sw, w_uk2, w_uv2, w_o = _prep_odd(w_in_odd[i], w_uq[i], w_uk[i], w_uv[i], w_o_odd[i])
            h = mm(xb, w_in, F32)
            cq, ckv, ckv_b, kpe, kpe_b = mla_post(h, cs_k, q_norm_g[i], kv_norm_g[i])
            qn = mm(cq, w_n, BF16, scale=MLA_SCALE)
            qp = q_rope(cq, w_pe, w_pe_sw, cs_q)
            kn = mm(ckv_b, w_uk2, BF16, rows=Tp)
            tq = _pick(L, (512, 256, 128))
            vt = value_blocks_t(ckv_b, w_uv2.T, rows=Tp, tk=tq)
            o_p = mla_prompt_attention(qn, qp, kn, kpe_b, vt, batch=B, seq=L, tq=tq)
            q_lat = head_mm(qn, w_uk2, MLA_HEADS, QK_NOPE, KV_LORA, transpose_w=True, row_start=Tp, rows=Ts)
            o_lat = mla_sample_attention(page_table, q_lat.reshape(Ts, MLA_HEADS, KV_LORA),
                                         qp.reshape(T, MLA_HEADS, LANES), ckv, kpe, cache_ckv2, cache_kpe2,
                                         i * n_pool, row_start=Tp, dec_seq=Ls)
            o_s = head_mm(o_lat.reshape(Ts, MLA_HEADS * KV_LORA), w_uv2, MLA_HEADS, KV_LORA, V_HEAD, transpose_w=False)
            y = mm(jnp.concatenate([o_p, o_s], axis=0), w_o, F32)
            ckv_out.append(ckv)
            kpe_out.append(kpe[:, :QK_ROPE])
        x, xb = residual_ln(x, y, ln_mix_g[l], ln_mix_b[l], dn_alpha)
        q = mm(xb, w_peer_q[l].astype(BF16), BF16)
        keys = peer_keys[l].reshape(PEER_HEADS * 2, PEER_NKEYS, -1).astype(BF16)
        k1, c1, r2, e2 = peer_topk(q, keys)
        y = peer_dense(xb, peer_u_b, peer_vt_b, l, k1, c1, r2, e2)
        x, xb = residual_ln(x, y, ln_ffn_g[l], ln_ffn_b[l], dn_alpha)

    def split(rows_list, width):
        p = jnp.stack([r[:Tp].reshape(B, L, width) for r in rows_list])
        s = jnp.stack([r[Tp:].reshape(Bs, Ls, width) for r in rows_list])
        return p, s

    ckv_p, ckv_s = split(ckv_out, KV_LORA)
    kpe_p, kpe_s = split(kpe_out, QK_ROPE)
    return (x[:Tp].reshape(B, L, D), x[Tp:].reshape(Bs, Ls, D),
            jnp.stack(gla_p), jnp.stack(gla_s), jnp.stack(gmlp_s),
            ckv_p, kpe_p, ckv_s, kpe_s)
```
